```python
import math
import jax, jax.numpy as jnp
from jax import lax
import numpy as np

D_MODEL = 2048
BATCH = 2
SEQ = 16384
DEPTH = 1

GRID_W = 64
PLE_DIM = 256
MIX_W = D_MODEL
RWKV_W = MIX_W // 2
NAT_W = MIX_W - RWKV_W
HEAD_SIZE = 64
RWKV_HEADS = RWKV_W // HEAD_SIZE
NAT_HEADS = NAT_W // HEAD_SIZE
DECAY_LORA = 64
ICLR_LORA = 64
N_DIR = 2
NAT_KH = 8
NAT_KW = 16
NORM_EPS = 1e-6
LNX_EPS = 64e-5
DECAY_SCALE = math.exp(-0.5)

O_R = 0
O_K = RWKV_W
O_V = 2 * RWKV_W
O_WD = 3 * RWKV_W
O_AD = O_WD + N_DIR * DECAY_LORA
SHIFT_W = O_AD + N_DIR * ICLR_LORA
O_G_RWKV = SHIFT_W
O_NAT = SHIFT_W + RWKV_W
IN_W = O_NAT + 4 * NAT_W

kernel_name = "hybrid_rwkv7_natten2d_bidir_block"


def rms_norm(x, g):
    xf = x.astype(jnp.float32)
    y = xf * lax.rsqrt(jnp.mean(xf * xf, axis=-1, keepdims=True) + NORM_EPS)
    return (y * g.astype(jnp.float32)).astype(x.dtype)


def centred_shift_mix(z, mu_prev, mu_next):
    prev = jnp.pad(z[:, :-1], ((0, 0), (1, 0), (0, 0)))
    nxt = jnp.pad(z[:, 1:], ((0, 0), (0, 1), (0, 0)))
    return z + mu_prev * (prev - z) + mu_next * (nxt - z)


def wkv7_scan(r, w, k, v, a_vec, b_vec, reverse):
    bsz, _, nh, n = r.shape
    xs = tuple(jnp.moveaxis(t, 1, 0) for t in (r, w, k, v, a_vec, b_vec))

    def step(state, inp):
        r_t, w_t, k_t, v_t, a_t, b_t = inp
        sa = jnp.einsum('bhvk,bhk->bhv', state, a_t)
        state = (state * w_t[:, :, None, :]
                 + sa[..., None] * b_t[:, :, None, :]
                 + v_t[..., None] * k_t[:, :, None, :])
        y = jnp.einsum('bhvk,bhk->bhv', state, r_t)
        return state, y

    s0 = jnp.zeros((bsz, nh, n, n), jnp.float32)
    _, ys = lax.scan(step, s0, xs, reverse=reverse)
    return jnp.moveaxis(ys, 0, 1)


def rwkv7_branch(zs, gate, w0, w2, a0, a2, k_k, k_a, r_k, lnx_w, lnx_b):
    bsz, t_len, _ = zs.shape
    zf = zs.astype(jnp.float32)
    hd = (bsz, t_len, RWKV_HEADS, HEAD_SIZE)
    r = zf[..., O_R:O_K]
    k = zf[..., O_K:O_V]
    v = zf[..., O_V:O_WD]
    wd = zf[..., O_WD:O_AD].reshape(bsz, t_len, N_DIR, DECAY_LORA)
    ad = zf[..., O_AD:SHIFT_W].reshape(bsz, t_len, N_DIR, ICLR_LORA)
    f32 = lambda a: a.astype(jnp.float32)
    decay = jnp.exp(-DECAY_SCALE * jax.nn.sigmoid(
        f32(w0) + jnp.einsum('btdr,drc->btdc', jnp.tanh(wd), f32(w2))))
    iclr = jax.nn.sigmoid(f32(a0) + jnp.einsum('btdr,drc->btdc', ad, f32(a2)))
    kk = (k * f32(k_k)).reshape(hd)
    kk = kk / jnp.maximum(jnp.sqrt(jnp.sum(kk * kk, axis=-1, keepdims=True)), 1e-12)
    k_dir = k[:, :, None, :] * (1.0 + (iclr - 1.0) * f32(k_a))
    rh, vh = r.reshape(hd), v.reshape(hd)
    ys = []
    for d in range(N_DIR):
        ys.append(wkv7_scan(rh, decay[:, :, d].reshape(hd), k_dir[:, :, d].reshape(hd), vh,
                            -kk, kk * iclr[:, :, d].reshape(hd), reverse=(d == 1)))
    y = ys[0] + ys[1]
    mu = jnp.mean(y, axis=-1, keepdims=True)
    var = jnp.mean(jnp.square(y - mu), axis=-1, keepdims=True)
    y = (y - mu) * lax.rsqrt(var + LNX_EPS)
    y = y * f32(lnx_w).reshape(RWKV_HEADS, HEAD_SIZE) + f32(lnx_b).reshape(RWKV_HEADS, HEAD_SIZE)
    bonus = jnp.sum(rh * k.reshape(hd) * f32(r_k), axis=-1, keepdims=True) * vh
    out = (y + bonus).reshape(bsz, t_len, RWKV_W) * jax.nn.silu(gate.astype(jnp.float32))
    return out.astype(zs.dtype)


def nat2d_branch(q, k, v, rpb):
    bsz, t_len, _ = q.shape
    rows = t_len // GRID_W
    kh = min(NAT_KH, rows)
    scale = HEAD_SIZE ** -0.5

    def to_grid(a):
        return a.reshape(bsz, rows, GRID_W, NAT_HEADS, HEAD_SIZE).transpose(0, 3, 1, 2, 4)

    qg, kg, vg = to_grid(q), to_grid(k), to_grid(v)
    row_start = jnp.clip(jnp.arange(rows) - kh // 2, 0, rows - kh)
    cols = jnp.arange(GRID_W)
    col_idx = jnp.clip(cols - NAT_KW // 2, 0, GRID_W - NAT_KW)[:, None] + jnp.arange(NAT_KW)[None, :]
    bias_c = col_idx - cols[:, None] + (NAT_KW - 1)

    def one_row(args):
        r, q_row = args
        rs = row_start[r]
        k_rows = lax.dynamic_slice_in_dim(kg, rs, kh, axis=2)
        v_rows = lax.dynamic_slice_in_dim(vg, rs, kh, axis=2)
        k_nb = k_rows[:, :, :, col_idx]
        v_nb = v_rows[:, :, :, col_idx]
        s = jnp.einsum('bhcd,bhicjd->bhcij', q_row, k_nb).astype(jnp.float32) * scale
        bias_r = rs + jnp.arange(kh) - r + (NAT_KH - 1)
        bias = rpb[:, bias_r[None, :, None], bias_c[:, None, :]]
        s = s + bias.astype(jnp.float32)[None]
        prob = jax.nn.softmax(s.reshape(bsz, NAT_HEADS, GRID_W, kh * NAT_KW), axis=-1)
        prob = prob.reshape(bsz, NAT_HEADS, GRID_W, kh, NAT_KW).astype(q.dtype)
        return jnp.einsum('bhcij,bhicjd->bhcd', prob, v_nb)

    out = lax.map(one_row, (jnp.arange(rows), jnp.moveaxis(qg, 2, 0)))
    return out.transpose(1, 0, 3, 2, 4).reshape(bsz, t_len, NAT_W)


def setup_inputs(seed: int = 0) -> dict:
    key = jax.random.key(seed)
    ks = jax.random.split(key, 24)
    nrm = lambda k, s: jax.random.normal(k, s, jnp.float32)
    return {
        "x": nrm(ks[0], (BATCH, SEQ, D_MODEL)),
        "p": nrm(ks[1], (DEPTH, BATCH, SEQ, PLE_DIM)),
        "norm_mix_g": 1.0 + 0.02 * nrm(ks[2], (DEPTH, D_MODEL)),
        "w_in": nrm(ks[3], (DEPTH, D_MODEL, IN_W)) * D_MODEL ** -0.5,
        "shift_mu_prev": jax.random.uniform(ks[4], (DEPTH, SHIFT_W), jnp.float32, 0.0, 0.5),
        "shift_mu_next": jax.random.uniform(ks[5], (DEPTH, SHIFT_W), jnp.float32, 0.0, 0.5),
        "decay_w0": jax.random.uniform(ks[6], (DEPTH, N_DIR, RWKV_W), jnp.float32, -6.0, 1.0),
        "decay_w2": nrm(ks[7], (DEPTH, N_DIR, DECAY_LORA, RWKV_W)) * 0.5 * DECAY_LORA ** -0.5,
        "iclr_a0": 0.5 * nrm(ks[8], (DEPTH, N_DIR, RWKV_W)),
        "iclr_a2": nrm(ks[9], (DEPTH, N_DIR, ICLR_LORA, RWKV_W)) * 0.5 * ICLR_LORA ** -0.5,
        "k_k": 0.85 + 0.05 * nrm(ks[10], (DEPTH, RWKV_W)),
        "k_a": 1.0 + 0.05 * nrm(ks[11], (DEPTH, RWKV_W)),
        "r_k": 0.1 * nrm(ks[12], (DEPTH, RWKV_HEADS, HEAD_SIZE)),
        "lnx_w": 1.0 + 0.02 * nrm(ks[13], (DEPTH, RWKV_W)),
        "lnx_b": 0.02 * nrm(ks[14], (DEPTH, RWKV_W)),
        "nat_rpb": 0.1 * nrm(ks[15], (DEPTH, NAT_HEADS, 2 * NAT_KH - 1, 2 * NAT_KW - 1)),
        "w_out": nrm(ks[16], (DEPTH, MIX_W, D_MODEL)) * MIX_W ** -0.5,
        "ple_norm_g": 1.0 + 0.02 * nrm(ks[17], (DEPTH, D_MODEL)),
        "w_ple_gate": nrm(ks[18], (DEPTH, D_MODEL, D_MODEL)) * D_MODEL ** -0.5,
        "w_ple_proj": nrm(ks[19], (DEPTH, PLE_DIM, D_MODEL)) * PLE_DIM ** -0.5,
        "final_norm_g": 1.0 + 0.02 * nrm(ks[20], (D_MODEL,)),
    }


def reference(x, p, norm_mix_g, w_in, shift_mu_prev, shift_mu_next, decay_w0, decay_w2,
              iclr_a0, iclr_a2, k_k, k_a, r_k, lnx_w, lnx_b, nat_rpb, w_out,
              ple_norm_g, w_ple_gate, w_ple_proj, final_norm_g):
    h = x
    for i in range(DEPTH):
        hn = rms_norm(h, norm_mix_g[i])
        z = hn @ w_in[i]
        zs = centred_shift_mix(z[..., :SHIFT_W], shift_mu_prev[i], shift_mu_next[i])
        y_a = rwkv7_branch(zs, z[..., O_G_RWKV:O_NAT], decay_w0[i], decay_w2[i],
                           iclr_a0[i], iclr_a2[i], k_k[i], k_a[i], r_k[i], lnx_w[i], lnx_b[i])
        qn = z[..., O_NAT:O_NAT + NAT_W]
        kn = z[..., O_NAT + NAT_W:O_NAT + 2 * NAT_W]
        vn = z[..., O_NAT + 2 * NAT_W:O_NAT + 3 * NAT_W]
        gn = z[..., O_NAT + 3 * NAT_W:O_NAT + 4 * NAT_W]
        y_b = nat2d_branch(qn, kn, vn, nat_rpb[i]) * jax.nn.silu(gn)
        h = h + jnp.concatenate([y_a, y_b], axis=-1) @ w_out[i]
        ple_gate = jax.nn.sigmoid(rms_norm(h, ple_norm_g[i]) @ w_ple_gate[i])
        h = h + (p[i] @ w_ple_proj[i]) * ple_gate
    return rms_norm(h, final_norm_g)
```

```python
import functools
import math

import numpy as np
import jax
import jax.numpy as jnp
from jax import lax
from jax.experimental import pallas as pl
from jax.experimental.pallas import tpu as pltpu

GRID_W = 64
HEAD = 64
RWKV_W = 1024
NAT_W = 1024
LORA = 64
NAT_KH = 8
NAT_KW = 16
NORM_EPS = 1e-6
LNX_EPS = 64e-5
DECAY_SCALE = math.exp(-0.5)

LANES = 128
PACK_ROWS = 16
VMEM_LIMIT = 56 * 1024 * 1024

CHUNK = 64
SUB = 16
PAIRS = RWKV_W // LANES
NEG = -1e30

F32 = jnp.float32
BF16 = jnp.bfloat16


def _dot(a, b):
    return jnp.dot(a, b, preferred_element_type=F32)


def _dot_nt(a, b):
    return lax.dot_general(a, b, (((1,), (1,)), ((), ())), preferred_element_type=F32)


def _dot_tn(a, b):
    return lax.dot_general(a, b, (((0,), (0,)), ((), ())), preferred_element_type=F32)


def _sigmoid(x):
    return 1.0 / (1.0 + jnp.exp(-x))


def _inproj_kernel(x_ref, xp_ref, xn_ref, g_ref, w_ref, wl_ref, mup_ref, mun_ref, mupl_ref, munl_ref,
                   zrkv_ref, grw_ref, qkv_ref, gn_ref, lora_ref, hn_ref, *, tm, tiles_per_batch):
    i = pl.program_id(0)
    j = pl.program_id(1)
    halo = PACK_ROWS
    rows = tm + 2 * halo

    def shifted(z, mup, mun):
        zp = pltpu.roll(z, 1, 0)[halo:halo + tm]
        zn = pltpu.roll(z, rows - 1, 0)[halo:halo + tm]
        zc = z[halo:halo + tm]
        return zc + mup * (zp - zc) + mun * (zn - zc)

    @pl.when(j == 0)
    def _():
        g = g_ref[...]

        def norm(xv):
            ms = jnp.mean(xv * xv, axis=-1, keepdims=True)
            return xv * lax.rsqrt(ms + NORM_EPS) * g

        ti = i % tiles_per_batch
        keep_p = (ti > 0).astype(F32)
        keep_n = (ti < tiles_per_batch - 1).astype(F32)
        hn_ref[0:halo] = (norm(xp_ref[...]) * keep_p).astype(BF16)
        hn_ref[halo:halo + tm] = norm(x_ref[...]).astype(BF16)
        hn_ref[halo + tm:rows] = (norm(xn_ref[...]) * keep_n).astype(BF16)
        zl = _dot(hn_ref[...], wl_ref[...])
        lora_ref[...] = shifted(zl, mupl_ref[...], munl_ref[...])

    @pl.when(j < 3)
    def _():
        z = _dot(hn_ref[...], w_ref[...])
        zs = shifted(z, mup_ref[0], mun_ref[0])
        for p in range(PAIRS):
            zrkv_ref[0, p] = zs[:, p * LANES:(p + 1) * LANES]

    @pl.when(j == 3)
    def _():
        z = _dot(hn_ref[halo:halo + tm], w_ref[...])
        for p in range(PAIRS):
            grw_ref[p] = z[:, p * LANES:(p + 1) * LANES]

    @pl.when((j >= 4) & (j < 7))
    def _():
        qkv_ref[0] = _dot(hn_ref[halo:halo + tm], w_ref[...]).astype(BF16)

    @pl.when(j == 7)
    def _():
        gn_ref[...] = _dot(hn_ref[halo:halo + tm], w_ref[...])


def _inproj(x2, norm_g, w_main, w_lora, mup, mun, mupl, munl, *, seq, tm):
    bt, d = x2.shape
    halo = PACK_ROWS
    tiles_per_batch = seq // tm
    n_i = bt // tm
    hb = tm // halo
    last_hb = bt // halo - 1
    kernel = functools.partial(_inproj_kernel, tm=tm, tiles_per_batch=tiles_per_batch)
    return pl.pallas_call(
        kernel,
        grid=(n_i, 8),
        in_specs=[
            pl.BlockSpec((tm, d), lambda i, j: (i, 0)),
            pl.BlockSpec((halo, d), lambda i, j: (jnp.maximum(i * hb - 1, 0), 0)),
            pl.BlockSpec((halo, d), lambda i, j: (jnp.minimum((i + 1) * hb, last_hb), 0)),
            pl.BlockSpec((1, d), lambda i, j: (0, 0)),
            pl.BlockSpec((d, RWKV_W), lambda i, j: (0, j)),
            pl.BlockSpec((d, 4 * LORA), lambda i, j: (0, 0)),
            pl.BlockSpec((1, 1, RWKV_W), lambda i, j: (jnp.minimum(j, 2), 0, 0)),
            pl.BlockSpec((1, 1, RWKV_W), lambda i, j: (jnp.minimum(j, 2), 0, 0)),
            pl.BlockSpec((1, 4 * LORA), lambda i, j: (0, 0)),
            pl.BlockSpec((1, 4 * LORA), lambda i, j: (0, 0)),
        ],
        out_specs=[
            pl.BlockSpec((1, PAIRS, tm, LANES), lambda i, j: (jnp.minimum(j, 2), 0, i, 0)),
            pl.BlockSpec((PAIRS, tm, LANES), lambda i, j: (0, i, 0)),
            pl.BlockSpec((1, tm, NAT_W), lambda i, j: (jnp.clip(j - 4, 0, 2), i, 0)),
            pl.BlockSpec((tm, NAT_W), lambda i, j: (i, 0)),
            pl.BlockSpec((tm, 4 * LORA), lambda i, j: (i, 0)),
        ],
        out_shape=[
            jax.ShapeDtypeStruct((3, PAIRS, bt, LANES), F32),
            jax.ShapeDtypeStruct((PAIRS, bt, LANES), F32),
            jax.ShapeDtypeStruct((3, bt, NAT_W), BF16),
            jax.ShapeDtypeStruct((bt, NAT_W), F32),
            jax.ShapeDtypeStruct((bt, 4 * LORA), F32),
        ],
        scratch_shapes=[pltpu.VMEM((tm + 2 * halo, d), BF16)],
        compiler_params=pltpu.CompilerParams(
            dimension_semantics=("arbitrary", "arbitrary"), vmem_limit_bytes=VMEM_LIMIT),
        name="inproj",
    )(x2, x2, x2, norm_g, w_main, w_lora, mup, mun, mupl, munl)


def _wkv_kernel(zf_ref, zr_ref, lf_ref, lr_ref, w0_ref, w2_ref, a0_ref, a2_ref, kk_ref, ka_ref, rk_ref,
                yf_ref, yr_ref, bonus_ref, st_ref, cinc_ref, lw_ref, iclr_ref):
    L = CHUNK
    W4 = 2 * LANES

    @pl.when(pl.program_id(1) == 0)
    def _():
        st_ref[...] = jnp.zeros_like(st_ref)

    row = lax.broadcasted_iota(jnp.int32, (L, L), 0)
    col = lax.broadcasted_iota(jnp.int32, (L, L), 1)
    for d, l_ref in ((0, lf_ref), (1, lr_ref)):
        lo = l_ref[...]
        wd = lo[:, d * LORA:(d + 1) * LORA]
        ad = lo[:, (2 + d) * LORA:(3 + d) * LORA]
        dw = _dot(jnp.tanh(wd).astype(BF16), w2_ref[d])
        lw = -DECAY_SCALE * _sigmoid(w0_ref[d] + dw)
        da = _dot(ad.astype(BF16), a2_ref[d])
        ic = _sigmoid(a0_ref[d] + da)
        tri = ((row >= col) if d == 0 else (col >= row)).astype(BF16)
        hi = lw.astype(BF16)
        lo2 = (lw - hi.astype(F32)).astype(BF16)
        cinc = _dot(tri, hi) + _dot(tri, lo2)
        for p in range(PAIRS):
            sl = slice(p * LANES, (p + 1) * LANES)
            cinc_ref[d, p] = cinc[:, sl]
            lw_ref[d, p] = lw[:, sl]
            iclr_ref[d, p] = ic[:, sl]

    ri = lax.broadcasted_iota(jnp.int32, (L, W4), 0)
    li = lax.broadcasted_iota(jnp.int32, (L, W4), 1)
    blk = li // HEAD
    jj = li % HEAD
    fwd = li < LANES
    strict = (fwd & (jj < ri)) | (~fwd & (jj > ri))
    incl = (fwd & (jj <= ri)) | (~fwd & (jj >= ri))
    eye = (jj == ri).astype(F32)
    diag_blk = (jj // SUB) == (ri // SUB)
    r4 = lax.broadcasted_iota(jnp.int32, (W4, W4), 0)
    c4 = lax.broadcasted_iota(jnp.int32, (W4, W4), 1)
    bd_mask = (r4 // HEAD) == (c4 // HEAD)
    r1 = lax.broadcasted_iota(jnp.int32, (LANES, LANES), 0)
    c1 = lax.broadcasted_iota(jnp.int32, (LANES, LANES), 1)
    ones_bd = ((r1 // HEAD) == (c1 // HEAD)).astype(BF16)

    def bd(x):
        xb = x.astype(BF16)
        z = jnp.zeros_like(xb)
        return jnp.concatenate([jnp.where(blk == q, xb, z) for q in range(4)], axis=0)

    def mm(a, b):
        return _dot(a.astype(BF16), bd(b))

    def pair_body(p, carry):
        kkp = kk_ref[p]
        kap = ka_ref[p]
        rkp = rk_ref[p]
        rt, at, bt, kt, bh, kh, vv, pl_tot = [], [], [], [], [], [], [], []
        for d, z_ref in ((0, zf_ref), (1, zr_ref)):
            r = z_ref[0, p]
            k = z_ref[1, p]
            v = z_ref[2, p]
            ci = cinc_ref[d, p]
            lw = lw_ref[d, p]
            ic = iclr_ref[d, p]
            kraw = k * kkp
            ss = _dot((kraw * kraw).astype(BF16), ones_bd)
            kk = kraw * lax.rsqrt(jnp.maximum(ss, 1e-24))
            kd = k * (1.0 + (ic - 1.0) * kap)
            b = kk * ic
            ctot = ci[L - 1:L] if d == 0 else ci[0:1]
            pinv = jnp.exp(-ci)
            phat = jnp.exp(ctot - ci)
            rt.append(r * jnp.exp(ci))
            at.append(-kk * jnp.exp(ci - lw))
            bt.append(b * pinv)
            kt.append(kd * pinv)
            bh.append(b * phat)
            kh.append(kd * phat)
            vv.append(v)
            pl_tot.append(jnp.exp(ctot))
            if d == 0:
                rk = _dot((r * k * rkp).astype(BF16), ones_bd)
                bonus_ref[p] = rk * v

        cat = lambda xs: jnp.concatenate(xs, axis=1)
        rt, at, bt, kt, bh, kh, vv, pl_tot = map(cat, (rt, at, bt, kt, bh, kh, vv, pl_tot))

        lhs = jnp.concatenate([at, rt], axis=0).astype(BF16)
        rhs = jnp.concatenate([bd(bt), bd(kt)], axis=0)
        o1 = _dot_nt(lhs, rhs)
        a_ab = jnp.where(strict, o1[0:L, 0:W4], 0.0)
        a_ak = jnp.where(strict, o1[0:L, W4:2 * W4], 0.0)
        a_rb = jnp.where(incl, o1[L:2 * L, 0:W4], 0.0)
        a_rk = jnp.where(incl, o1[L:2 * L, W4:2 * W4], 0.0)

        dg = jnp.where(diag_blk, a_ab, 0.0)
        off = a_ab - dg
        t0 = eye + dg
        p1 = mm(dg, dg)
        o = mm(jnp.concatenate([t0, p1], axis=0), p1)
        t1 = t0 + o[0:L]
        p2 = o[L:2 * L]
        o = mm(jnp.concatenate([t1, p2], axis=0), p2)
        t2 = t1 + o[0:L]
        p3 = o[L:2 * L]
        td = t2 + mm(t2, p3)
        f = mm(td, off)
        f2 = mm(f, f)
        x = td + mm(f, td)
        tinv = x + mm(f2, x)

        av = mm(a_ak, vv)
        o4 = _dot(tinv.astype(BF16), jnp.concatenate([bd(at), bd(av)], axis=1))
        wa = o4[:, 0:W4]
        uv = o4[:, W4:2 * W4]
        rw = rt + mm(a_rb, wa)
        yl = _dot(jnp.concatenate([a_rb, a_rk], axis=1).astype(BF16),
                  jnp.concatenate([bd(uv), bd(vv)], axis=0))
        m_full = _dot_tn(wa.astype(BF16), bh.astype(BF16))
        g_full = _dot_tn(jnp.concatenate([uv, vv], axis=0).astype(BF16),
                         jnp.concatenate([bh, kh], axis=0).astype(BF16))
        mm_bd = jnp.where(bd_mask, m_full, 0.0).astype(BF16)
        g_bd = jnp.where(bd_mask, g_full, 0.0)

        s = st_ref[p]
        sb = s.astype(BF16)
        y = _dot_nt(rw.astype(BF16), sb) + yl
        st_ref[p] = s * pl_tot + _dot(sb, mm_bd) + g_bd
        yf_ref[p] = y[:, 0:LANES]
        yr_ref[p] = y[:, LANES:2 * LANES]
        return carry

    lax.fori_loop(0, PAIRS, pair_body, 0)


def _wkv(zrkv, lora, w0, w2, a0, a2, k_k, k_a, r_k, *, batch, seq):
    bt = zrkv.shape[2]
    nc = seq // CHUNK
    fidx = lambda b, i: b * nc + i
    ridx = lambda b, i: b * nc + (nc - 1 - i)
    full = lambda shape: pl.BlockSpec(shape, lambda b, i: (0,) * len(shape))
    return pl.pallas_call(
        _wkv_kernel,
        grid=(batch, nc),
        in_specs=[
            pl.BlockSpec((3, PAIRS, CHUNK, LANES), lambda b, i: (0, 0, fidx(b, i), 0)),
            pl.BlockSpec((3, PAIRS, CHUNK, LANES), lambda b, i: (0, 0, ridx(b, i), 0)),
            pl.BlockSpec((CHUNK, 4 * LORA), lambda b, i: (fidx(b, i), 0)),
            pl.BlockSpec((CHUNK, 4 * LORA), lambda b, i: (ridx(b, i), 0)),
            full((2, 1, RWKV_W)), full((2, LORA, RWKV_W)),
            full((2, 1, RWKV_W)), full((2, LORA, RWKV_W)),
            full((PAIRS, 1, LANES)), full((PAIRS, 1, LANES)), full((PAIRS, 1, LANES)),
        ],
        out_specs=[
            pl.BlockSpec((PAIRS, CHUNK, LANES), lambda b, i: (0, fidx(b, i), 0)),
            pl.BlockSpec((PAIRS, CHUNK, LANES), lambda b, i: (0, ridx(b, i), 0)),
            pl.BlockSpec((PAIRS, CHUNK, LANES), lambda b, i: (0, fidx(b, i), 0)),
        ],
        out_shape=[jax.ShapeDtypeStruct((PAIRS, bt, LANES), F32)] * 3,
        scratch_shapes=[
            pltpu.VMEM((PAIRS, 2 * LANES, 2 * LANES), F32),
            pltpu.VMEM((2, PAIRS, CHUNK, LANES), F32),
            pltpu.VMEM((2, PAIRS, CHUNK, LANES), F32),
            pltpu.VMEM((2, PAIRS, CHUNK, LANES), F32),
        ],
        compiler_params=pltpu.CompilerParams(
            dimension_semantics=("arbitrary", "arbitrary"), vmem_limit_bytes=VMEM_LIMIT),
        name="wkv7",
    )(zrkv, zrkv, lora, lora, w0, w2, a0, a2, k_k, k_a, r_k)


NAT_ROWS_PER_STEP = 8
NAT_GROUP_W = 2 * LANES


def _nat_kernel(q_ref, k_ref, v_ref, g_ref, bias_ref, o_ref, *, rows):
    rb = pl.program_id(2)
    kh = min(NAT_KH, rows)
    win = kh * GRID_W
    scale = HEAD ** -0.5
    lane = lax.broadcasted_iota(jnp.int32, (GRID_W, LANES), 1)
    low = lane < HEAD

    def row_body(qr, carry):
        r = rb * NAT_ROWS_PER_STEP + qr
        rs = jnp.clip(r - kh // 2, 0, rows - kh)
        pidx = rs - r + (NAT_KH - 1)
        q0 = pl.multiple_of(qr * GRID_W, GRID_W)
        k0 = pl.multiple_of(rs * GRID_W, GRID_W)
        for pp in range(NAT_GROUP_W // LANES):
            sl = slice(pp * LANES, (pp + 1) * LANES)
            qp = q_ref[0, pl.ds(q0, GRID_W), sl]
            kp = k_ref[0, pl.ds(k0, win), sl]
            vp = v_ref[0, pl.ds(k0, win), sl]
            outs = []
            for hh in range(2):
                qm = jnp.where(low if hh == 0 else ~low, qp, jnp.zeros_like(qp))
                s = _dot_nt(qm, kp) * scale + bias_ref[pidx, 2 * pp + hh]
                m = jnp.max(s, axis=-1, keepdims=True)
                e = jnp.exp(s - m)
                l = jnp.sum(e, axis=-1, keepdims=True)
                outs.append(_dot(e.astype(BF16), vp) / l)
            o = jnp.where(low, outs[0], outs[1])
            g = g_ref[pl.ds(q0, GRID_W), sl]
            o_ref[pl.ds(q0, GRID_W), sl] = (o * (g * _sigmoid(g))).astype(o_ref.dtype)
        return carry

    lax.fori_loop(0, NAT_ROWS_PER_STEP, row_body, 0)


def _nat(qkv, gn, bias, *, batch, seq):
    bt = gn.shape[0]
    rows = seq // GRID_W
    kh = min(NAT_KH, rows)
    tq = NAT_ROWS_PER_STEP * GRID_W
    nrb = rows // NAT_ROWS_PER_STEP
    ngrp = NAT_W // NAT_GROUP_W
    hpg = NAT_GROUP_W // HEAD
    kernel = functools.partial(_nat_kernel, rows=rows)
    return pl.pallas_call(
        kernel,
        grid=(batch, ngrp, nrb),
        in_specs=[
            pl.BlockSpec((1, tq, NAT_GROUP_W), lambda b, g, r: (0, b * nrb + r, g)),
            pl.BlockSpec((1, seq, NAT_GROUP_W), lambda b, g, r: (1, b, g)),
            pl.BlockSpec((1, seq, NAT_GROUP_W), lambda b, g, r: (2, b, g)),
            pl.BlockSpec((tq, NAT_GROUP_W), lambda b, g, r: (b * nrb + r, g)),
            pl.BlockSpec((NAT_KH, hpg, GRID_W, kh * GRID_W), lambda b, g, r: (0, g, 0, 0)),
        ],
        out_specs=pl.BlockSpec((tq, NAT_GROUP_W), lambda b, g, r: (b * nrb + r, g)),
        out_shape=jax.ShapeDtypeStruct((bt, NAT_W), BF16),
        compiler_params=pltpu.CompilerParams(
            dimension_semantics=("arbitrary", "arbitrary", "arbitrary"), vmem_limit_bytes=VMEM_LIMIT),
        name="nat2d",
    )(qkv, qkv, qkv, gn, bias)


def _nat_bias_table(rpb, rows):
    kh = min(NAT_KH, rows)
    cols = np.arange(GRID_W)
    col_start = np.clip(cols - NAT_KW // 2, 0, GRID_W - NAT_KW)
    rel = cols[None, :] - cols[:, None]
    valid = (cols[None, :] >= col_start[:, None]) & (cols[None, :] < col_start[:, None] + NAT_KW)
    cidx = np.clip(rel + NAT_KW - 1, 0, 2 * NAT_KW - 2)
    ridx = np.minimum(np.arange(NAT_KH)[:, None] + np.arange(kh)[None, :], 2 * NAT_KH - 2)
    t = rpb[:, ridx][:, :, :, cidx]
    t = jnp.where(valid[None, None, None], t.astype(F32), NEG)
    t = jnp.transpose(t, (1, 0, 3, 2, 4))
    return t.reshape(NAT_KH, rpb.shape[0], GRID_W, kh * GRID_W)


def _out_kernel(yf_ref, yr_ref, bonus_ref, grw_ref, yb_ref, x_ref, p_ref, lnw_ref, lnb_ref,
                wout_ref, pg_ref, wgate_ref, wproj_ref, fg_ref, o_ref):
    r1 = lax.broadcasted_iota(jnp.int32, (LANES, LANES), 0)
    c1 = lax.broadcasted_iota(jnp.int32, (LANES, LANES), 1)
    avg_bd = jnp.where((r1 // HEAD) == (c1 // HEAD), 1.0 / HEAD, 0.0).astype(BF16)
    parts = []
    for p in range(PAIRS):
        y = yf_ref[p] + yr_ref[p]
        mu = _dot(y.astype(BF16), avg_bd)
        dlt = y - mu
        var = _dot((dlt * dlt).astype(BF16), avg_bd)
        yn = dlt * lax.rsqrt(var + LNX_EPS) * lnw_ref[p] + lnb_ref[p]
        g = grw_ref[p]
        parts.append(((yn + bonus_ref[p]) * (g * _sigmoid(g))).astype(BF16))
    parts.append(yb_ref[...])
    mix = jnp.concatenate(parts, axis=1)
    h = x_ref[...] + _dot(mix, wout_ref[...])

    def rms(v, gain):
        ms = jnp.mean(v * v, axis=-1, keepdims=True)
        return v * lax.rsqrt(ms + NORM_EPS) * gain

    gate = _sigmoid(_dot(rms(h, pg_ref[...]).astype(BF16), wgate_ref[...]))
    h = h + _dot(p_ref[...].astype(BF16), wproj_ref[...]) * gate
    o_ref[...] = rms(h, fg_ref[...])


def _out_stage(yf, yr, bonus, grw, yb, x2, p2, lnw, lnb, w_out, ple_g, w_gate, w_proj, final_g, *, tm):
    bt, d = x2.shape
    ple = p2.shape[1]
    const = lambda shape: pl.BlockSpec(shape, lambda i: (0,) * len(shape), pipeline_mode=pl.Buffered(1))
    pair_spec = pl.BlockSpec((PAIRS, tm, LANES), lambda i: (0, i, 0))
    return pl.pallas_call(
        _out_kernel,
        grid=(bt // tm,),
        in_specs=[
            pair_spec, pair_spec, pair_spec, pair_spec,
            pl.BlockSpec((tm, NAT_W), lambda i: (i, 0)),
            pl.BlockSpec((tm, d), lambda i: (i, 0)),
            pl.BlockSpec((tm, ple), lambda i: (i, 0)),
            const((PAIRS, 1, LANES)), const((PAIRS, 1, LANES)),
            const((RWKV_W + NAT_W, d)), const((1, d)), const((d, d)), const((ple, d)), const((1, d)),
        ],
        out_specs=pl.BlockSpec((tm, d), lambda i: (i, 0)),
        out_shape=jax.ShapeDtypeStruct((bt, d), F32),
        compiler_params=pltpu.CompilerParams(
            dimension_semantics=("arbitrary",), vmem_limit_bytes=VMEM_LIMIT),
        name="out_stage",
    )(yf, yr, bonus, grw, yb, x2, p2, lnw, lnb, w_out, ple_g, w_gate, w_proj, final_g)


INPROJ_TM = 512
OUT_TM = 256


def kernel(x, p, norm_mix_g, w_in, shift_mu_prev, shift_mu_next, decay_w0, decay_w2, iclr_a0, iclr_a2,
           k_k, k_a, r_k, lnx_w, lnx_b, nat_rpb, w_out, ple_norm_g, w_ple_gate, w_ple_proj, final_norm_g):
    batch, seq, d = x.shape
    depth = p.shape[0]
    bt = batch * seq
    rows = seq // GRID_W
    o_wd = 3 * RWKV_W
    shift_w = o_wd + 4 * LORA
    o_nat = shift_w + RWKV_W
    pairs = lambda a: a.reshape(PAIRS, 1, LANES).astype(F32)

    h = x.reshape(bt, d)
    for i in range(depth):
        w = w_in[i]
        w_main = jnp.concatenate([w[:, :o_wd], w[:, shift_w:]], axis=1).astype(BF16)
        w_lora = w[:, o_wd:shift_w].astype(BF16)
        mup, mun = shift_mu_prev[i], shift_mu_next[i]
        zrkv, grw, qkv, gn, lora = _inproj(
            h, norm_mix_g[i].reshape(1, d), w_main, w_lora,
            mup[:o_wd].reshape(3, 1, RWKV_W), mun[:o_wd].reshape(3, 1, RWKV_W),
            mup[o_wd:].reshape(1, 4 * LORA), mun[o_wd:].reshape(1, 4 * LORA),
            seq=seq, tm=INPROJ_TM)
        yf, yr, bonus = _wkv(
            zrkv, lora,
            decay_w0[i].reshape(2, 1, RWKV_W), decay_w2[i].astype(BF16),
            iclr_a0[i].reshape(2, 1, RWKV_W), iclr_a2[i].astype(BF16),
            pairs(k_k[i]), pairs(k_a[i]), pairs(r_k[i]), batch=batch, seq=seq)
        yb = _nat(qkv, gn, _nat_bias_table(nat_rpb[i], rows), batch=batch, seq=seq)
        h = _out_stage(
            yf, yr, bonus, grw, yb, h, p[i].reshape(bt, -1), pairs(lnx_w[i]), pairs(lnx_b[i]),
            w_out[i].astype(BF16), ple_norm_g[i].reshape(1, d), w_ple_gate[i].astype(BF16),
            w_ple_proj[i].astype(BF16),
            final_norm_g.reshape(1, d) if i == depth - 1 else jnp.ones((1, d), F32), tm=OUT_TM)
    return h.reshape(batch, seq, d)
```

```python
import functools
import math

import numpy as np
import jax
import jax.numpy as jnp
from jax import lax
from jax.experimental import pallas as pl
from jax.experimental.pallas import tpu as pltpu

GRID_W = 64
HEAD = 64
RWKV_W = 1024
NAT_W = 1024
LORA = 64
NAT_KH = 8
NAT_KW = 16
NORM_EPS = 1e-6
LNX_EPS = 64e-5
DECAY_SCALE = math.exp(-0.5)

LANES = 128
PACK_ROWS = 16
VMEM_LIMIT = 56 * 1024 * 1024

CHUNK = 64
SUB = 16
PAIRS = RWKV_W // LANES
NEG = -1e30

F32 = jnp.float32
BF16 = jnp.bfloat16


def _dot(a, b):
    return jnp.dot(a, b, preferred_element_type=F32)


def _dot_nt(a, b):
    return lax.dot_general(a, b, (((1,), (1,)), ((), ())), preferred_element_type=F32)


def _dot_tn(a, b):
    return lax.dot_general(a, b, (((0,), (0,)), ((), ())), preferred_element_type=F32)


def _sigmoid(x):
    return 1.0 / (1.0 + jnp.exp(-x))


def _round_robin(chains):
    chains = list(chains)
    while chains:
        alive = []
        for c in chains:
            try:
                next(c)
                alive.append(c)
            except StopIteration:
                pass
        chains = alive


def _inproj_kernel(x_ref, xp_ref, xn_ref, g_ref, w_ref, wl_ref, mup_ref, mun_ref, mupl_ref, munl_ref,
                   zrkv_ref, grw_ref, qkv_ref, gn_ref, lora_ref, hn_ref, *, tm, tiles_per_batch):
    i = pl.program_id(0)
    j = pl.program_id(1)
    halo = PACK_ROWS
    rows = tm + 2 * halo

    def shifted(z, mup, mun):
        zp = pltpu.roll(z, 1, 0)[halo:halo + tm]
        zn = pltpu.roll(z, rows - 1, 0)[halo:halo + tm]
        zc = z[halo:halo + tm]
        return zc + mup * (zp - zc) + mun * (zn - zc)

    @pl.when(j == 0)
    def _():
        g = g_ref[...]

        def norm(xv):
            ms = jnp.mean(xv * xv, axis=-1, keepdims=True)
            return xv * lax.rsqrt(ms + NORM_EPS) * g

        ti = i % tiles_per_batch
        keep_p = (ti > 0).astype(F32)
        keep_n = (ti < tiles_per_batch - 1).astype(F32)
        hn_ref[0:halo] = (norm(xp_ref[...]) * keep_p).astype(BF16)
        hn_ref[halo:halo + tm] = norm(x_ref[...]).astype(BF16)
        hn_ref[halo + tm:rows] = (norm(xn_ref[...]) * keep_n).astype(BF16)
        zl = _dot(hn_ref[...], wl_ref[...])
        lora_ref[...] = shifted(zl, mupl_ref[...], munl_ref[...])

    @pl.when(j < 3)
    def _():
        z = _dot(hn_ref[...], w_ref[...])
        zs = shifted(z, mup_ref[0], mun_ref[0])
        for p in range(PAIRS):
            zrkv_ref[0, p] = zs[:, p * LANES:(p + 1) * LANES]

    @pl.when(j == 3)
    def _():
        z = _dot(hn_ref[halo:halo + tm], w_ref[...])
        for p in range(PAIRS):
            grw_ref[p] = z[:, p * LANES:(p + 1) * LANES]

    @pl.when((j >= 4) & (j < 7))
    def _():
        qkv_ref[0] = _dot(hn_ref[halo:halo + tm], w_ref[...]).astype(BF16)

    @pl.when(j == 7)
    def _():
        gn_ref[...] = _dot(hn_ref[halo:halo + tm], w_ref[...])


def _inproj(x2, norm_g, w_main, w_lora, mup, mun, mupl, munl, *, seq, tm):
    bt, d = x2.shape
    halo = PACK_ROWS
    tiles_per_batch = seq // tm
    n_i = bt // tm
    hb = tm // halo
    last_hb = bt // halo - 1
    kernel = functools.partial(_inproj_kernel, tm=tm, tiles_per_batch=tiles_per_batch)
    return pl.pallas_call(
        kernel,
        grid=(n_i, 8),
        in_specs=[
            pl.BlockSpec((tm, d), lambda i, j: (i, 0)),
            pl.BlockSpec((halo, d), lambda i, j: (jnp.maximum(i * hb - 1, 0), 0)),
            pl.BlockSpec((halo, d), lambda i, j: (jnp.minimum((i + 1) * hb, last_hb), 0)),
            pl.BlockSpec((1, d), lambda i, j: (0, 0)),
            pl.BlockSpec((d, RWKV_W), lambda i, j: (0, j)),
            pl.BlockSpec((d, 4 * LORA), lambda i, j: (0, 0)),
            pl.BlockSpec((1, 1, RWKV_W), lambda i, j: (jnp.minimum(j, 2), 0, 0)),
            pl.BlockSpec((1, 1, RWKV_W), lambda i, j: (jnp.minimum(j, 2), 0, 0)),
            pl.BlockSpec((1, 4 * LORA), lambda i, j: (0, 0)),
            pl.BlockSpec((1, 4 * LORA), lambda i, j: (0, 0)),
        ],
        out_specs=[
            pl.BlockSpec((1, PAIRS, tm, LANES), lambda i, j: (jnp.minimum(j, 2), 0, i, 0)),
            pl.BlockSpec((PAIRS, tm, LANES), lambda i, j: (0, i, 0)),
            pl.BlockSpec((1, tm, NAT_W), lambda i, j: (jnp.clip(j - 4, 0, 2), i, 0)),
            pl.BlockSpec((tm, NAT_W), lambda i, j: (i, 0)),
            pl.BlockSpec((tm, 4 * LORA), lambda i, j: (i, 0)),
        ],
        out_shape=[
            jax.ShapeDtypeStruct((3, PAIRS, bt, LANES), F32),
            jax.ShapeDtypeStruct((PAIRS, bt, LANES), F32),
            jax.ShapeDtypeStruct((3, bt, NAT_W), BF16),
            jax.ShapeDtypeStruct((bt, NAT_W), F32),
            jax.ShapeDtypeStruct((bt, 4 * LORA), F32),
        ],
        scratch_shapes=[pltpu.VMEM((tm + 2 * halo, d), BF16)],
        compiler_params=pltpu.CompilerParams(
            dimension_semantics=("arbitrary", "arbitrary"), vmem_limit_bytes=VMEM_LIMIT),
        name="inproj",
    )(x2, x2, x2, norm_g, w_main, w_lora, mup, mun, mupl, munl)


def _wkv_kernel(zf_ref, zr_ref, lf_ref, lr_ref, w0_ref, w2_ref, a0_ref, a2_ref, kk_ref, ka_ref, rk_ref,
                yf_ref, yr_ref, bonus_ref, st_ref, cinc_ref, lw_ref, iclr_ref):
    L = CHUNK
    W4 = 2 * LANES

    @pl.when(pl.program_id(1) == 0)
    def _():
        st_ref[...] = jnp.zeros_like(st_ref)

    row = lax.broadcasted_iota(jnp.int32, (L, L), 0)
    col = lax.broadcasted_iota(jnp.int32, (L, L), 1)
    for d, l_ref in ((0, lf_ref), (1, lr_ref)):
        lo = l_ref[...]
        wd = lo[:, d * LORA:(d + 1) * LORA]
        ad = lo[:, (2 + d) * LORA:(3 + d) * LORA]
        dw = _dot(jnp.tanh(wd).astype(BF16), w2_ref[d])
        lw = -DECAY_SCALE * _sigmoid(w0_ref[d] + dw)
        da = _dot(ad.astype(BF16), a2_ref[d])
        ic = _sigmoid(a0_ref[d] + da)
        tri = ((row >= col) if d == 0 else (col >= row)).astype(BF16)
        hi = lw.astype(BF16)
        lo2 = (lw - hi.astype(F32)).astype(BF16)
        cinc = _dot(tri, hi) + _dot(tri, lo2)
        for p in range(PAIRS):
            sl = slice(p * LANES, (p + 1) * LANES)
            cinc_ref[d, p] = cinc[:, sl]
            lw_ref[d, p] = lw[:, sl]
            iclr_ref[d, p] = ic[:, sl]

    ri = lax.broadcasted_iota(jnp.int32, (L, W4), 0)
    li = lax.broadcasted_iota(jnp.int32, (L, W4), 1)
    blk = li // HEAD
    jj = li % HEAD
    fwd = li < LANES
    strict = (fwd & (jj < ri)) | (~fwd & (jj > ri))
    incl = (fwd & (jj <= ri)) | (~fwd & (jj >= ri))
    eye = (jj == ri).astype(F32)
    diag_blk = (jj // SUB) == (ri // SUB)
    r4 = lax.broadcasted_iota(jnp.int32, (W4, W4), 0)
    c4 = lax.broadcasted_iota(jnp.int32, (W4, W4), 1)
    bd_mask = (r4 // HEAD) == (c4 // HEAD)
    r1 = lax.broadcasted_iota(jnp.int32, (LANES, LANES), 0)
    c1 = lax.broadcasted_iota(jnp.int32, (LANES, LANES), 1)
    ones_bd = ((r1 // HEAD) == (c1 // HEAD)).astype(BF16)

    def bd(x):
        xb = x.astype(BF16)
        z = jnp.zeros_like(xb)
        return jnp.concatenate([jnp.where(blk == q, xb, z) for q in range(4)], axis=0)

    def mm(a, b):
        return _dot(a.astype(BF16), bd(b))

    def pair_chain(p):
        kkp = kk_ref[p]
        kap = ka_ref[p]
        rkp = rk_ref[p]
        sums = []
        for d, z_ref in ((0, zf_ref), (1, zr_ref)):
            kraw = z_ref[1, p] * kkp
            sums.append(_dot((kraw * kraw).astype(BF16), ones_bd))
        rk = _dot((zf_ref[0, p] * zf_ref[1, p] * rkp).astype(BF16), ones_bd)
        yield
        bonus_ref[p] = rk * zf_ref[2, p]
        rt, at, bt, kt, bh, kh, vv, pl_tot = [], [], [], [], [], [], [], []
        for d, z_ref in ((0, zf_ref), (1, zr_ref)):
            r = z_ref[0, p]
            k = z_ref[1, p]
            v = z_ref[2, p]
            ci = cinc_ref[d, p]
            lw = lw_ref[d, p]
            ic = iclr_ref[d, p]
            kk = (k * kkp) * lax.rsqrt(jnp.maximum(sums[d], 1e-24))
            kd = k * (1.0 + (ic - 1.0) * kap)
            b = kk * ic
            ctot = ci[L - 1:L] if d == 0 else ci[0:1]
            pinv = jnp.exp(-ci)
            phat = jnp.exp(ctot - ci)
            rt.append(r * jnp.exp(ci))
            at.append(-kk * jnp.exp(ci - lw))
            bt.append(b * pinv)
            kt.append(kd * pinv)
            bh.append(b * phat)
            kh.append(kd * phat)
            vv.append(v)
            pl_tot.append(jnp.exp(ctot))

        cat = lambda xs: jnp.concatenate(xs, axis=1)
        rt, at, bt, kt, bh, kh, vv, pl_tot = map(cat, (rt, at, bt, kt, bh, kh, vv, pl_tot))

        lhs = jnp.concatenate([at, rt], axis=0).astype(BF16)
        rhs = jnp.concatenate([bd(bt), bd(kt)], axis=0)
        o1 = _dot_nt(lhs, rhs)
        yield
        a_ab = jnp.where(strict, o1[0:L, 0:W4], 0.0)
        a_ak = jnp.where(strict, o1[0:L, W4:2 * W4], 0.0)
        a_rb = jnp.where(incl, o1[L:2 * L, 0:W4], 0.0)
        a_rk = jnp.where(incl, o1[L:2 * L, W4:2 * W4], 0.0)

        dg = jnp.where(diag_blk, a_ab, 0.0)
        off = a_ab - dg
        t0 = eye + dg
        p1 = mm(dg, dg)
        av = mm(a_ak, vv)
        yield
        o = mm(jnp.concatenate([t0, p1], axis=0), p1)
        yield
        t1 = t0 + o[0:L]
        p2 = o[L:2 * L]
        o = mm(jnp.concatenate([t1, p2], axis=0), p2)
        yield
        t2 = t1 + o[0:L]
        p3 = o[L:2 * L]
        o = mm(t2, p3)
        yield
        td = t2 + o
        f = mm(td, off)
        yield
        f2 = mm(f, f)
        o = mm(f, td)
        yield
        x = td + o
        o = mm(f2, x)
        yield
        tinv = x + o
        o4 = _dot(tinv.astype(BF16), jnp.concatenate([bd(at), bd(av)], axis=1))
        yield
        wa = o4[:, 0:W4]
        uv = o4[:, W4:2 * W4]

        s = st_ref[p]
        o5 = _dot_nt(jnp.concatenate([wa, rt], axis=0).astype(BF16), s.astype(BF16))
        yield
        u = o5[0:L] + uv
        o6 = _dot(jnp.concatenate([a_rb, a_rk], axis=1).astype(BF16),
                  jnp.concatenate([bd(u), bd(vv)], axis=0))
        g_full = _dot_tn(jnp.concatenate([u, vv], axis=0).astype(BF16),
                         jnp.concatenate([bh, kh], axis=0).astype(BF16))
        yield
        y = o5[L:2 * L] + o6
        st_ref[p] = s * pl_tot + jnp.where(bd_mask, g_full, 0.0)
        yf_ref[p] = y[:, 0:LANES]
        yr_ref[p] = y[:, LANES:2 * LANES]

    _round_robin([pair_chain(p) for p in range(PAIRS)])


def _wkv(zrkv, lora, w0, w2, a0, a2, k_k, k_a, r_k, *, batch, seq):
    bt = zrkv.shape[2]
    nc = seq // CHUNK
    fidx = lambda b, i: b * nc + i
    ridx = lambda b, i: b * nc + (nc - 1 - i)
    full = lambda shape: pl.BlockSpec(shape, lambda b, i: (0,) * len(shape))
    return pl.pallas_call(
        _wkv_kernel,
        grid=(batch, nc),
        in_specs=[
            pl.BlockSpec((3, PAIRS, CHUNK, LANES), lambda b, i: (0, 0, fidx(b, i), 0)),
            pl.BlockSpec((3, PAIRS, CHUNK, LANES), lambda b, i: (0, 0, ridx(b, i), 0)),
            pl.BlockSpec((CHUNK, 4 * LORA), lambda b, i: (fidx(b, i), 0)),
            pl.BlockSpec((CHUNK, 4 * LORA), lambda b, i: (ridx(b, i), 0)),
            full((2, 1, RWKV_W)), full((2, LORA, RWKV_W)),
            full((2, 1, RWKV_W)), full((2, LORA, RWKV_W)),
            full((PAIRS, 1, LANES)), full((PAIRS, 1, LANES)), full((PAIRS, 1, LANES)),
        ],
        out_specs=[
            pl.BlockSpec((PAIRS, CHUNK, LANES), lambda b, i: (0, fidx(b, i), 0)),
            pl.BlockSpec((PAIRS, CHUNK, LANES), lambda b, i: (0, ridx(b, i), 0)),
            pl.BlockSpec((PAIRS, CHUNK, LANES), lambda b, i: (0, fidx(b, i), 0)),
        ],
        out_shape=[jax.ShapeDtypeStruct((PAIRS, bt, LANES), F32)] * 3,
        scratch_shapes=[
            pltpu.VMEM((PAIRS, 2 * LANES, 2 * LANES), F32),
            pltpu.VMEM((2, PAIRS, CHUNK, LANES), F32),
            pltpu.VMEM((2, PAIRS, CHUNK, LANES), F32),
            pltpu.VMEM((2, PAIRS, CHUNK, LANES), F32),
        ],
        compiler_params=pltpu.CompilerParams(
            dimension_semantics=("arbitrary", "arbitrary"), vmem_limit_bytes=VMEM_LIMIT),
        name="wkv7",
    )(zrkv, zrkv, lora, lora, w0, w2, a0, a2, k_k, k_a, r_k)


NAT_ROWS_PER_STEP = 8
NAT_GROUP_W = 2 * LANES


def _nat_kernel(q_ref, k_ref, v_ref, g_ref, bias_ref, o_ref, *, rows):
    rb = pl.program_id(2)
    kh = min(NAT_KH, rows)
    win = kh * GRID_W
    scale = HEAD ** -0.5
    lane = lax.broadcasted_iota(jnp.int32, (GRID_W, LANES), 1)
    low = lane < HEAD

    def chain(qr, pp):
        r = rb * NAT_ROWS_PER_STEP + qr
        rs = jnp.clip(r - kh // 2, 0, rows - kh)
        pidx = rs - r + (NAT_KH - 1)
        q0 = qr * GRID_W
        k0 = pl.multiple_of(rs * GRID_W, GRID_W)
        sl = slice(pp * LANES, (pp + 1) * LANES)
        qp = q_ref[0, q0:q0 + GRID_W, sl]
        zq = jnp.zeros_like(qp)
        qs = jnp.concatenate([jnp.where(low, qp, zq), jnp.where(low, zq, qp)], axis=0)
        s = _dot_nt(qs, k_ref[0, pl.ds(k0, win), sl])
        yield
        s = s * scale + bias_ref[pidx, pp]
        m = jnp.max(s, axis=-1, keepdims=True)
        e = jnp.exp(s - m)
        l = jnp.sum(e, axis=-1, keepdims=True)
        o2 = _dot(e.astype(BF16), v_ref[0, pl.ds(k0, win), sl])
        yield
        o2 = o2 / l
        o = jnp.where(low, o2[0:GRID_W], o2[GRID_W:2 * GRID_W])
        g = g_ref[q0:q0 + GRID_W, sl]
        o_ref[q0:q0 + GRID_W, sl] = (o * (g * _sigmoid(g))).astype(o_ref.dtype)

    _round_robin([chain(qr, pp) for qr in range(NAT_ROWS_PER_STEP)
                  for pp in range(NAT_GROUP_W // LANES)])


def _nat(qkv, gn, bias, *, batch, seq):
    bt = gn.shape[0]
    rows = seq // GRID_W
    kh = min(NAT_KH, rows)
    tq = NAT_ROWS_PER_STEP * GRID_W
    nrb = rows // NAT_ROWS_PER_STEP
    ngrp = NAT_W // NAT_GROUP_W
    ppg = NAT_GROUP_W // LANES
    kernel = functools.partial(_nat_kernel, rows=rows)
    return pl.pallas_call(
        kernel,
        grid=(batch, ngrp, nrb),
        in_specs=[
            pl.BlockSpec((1, tq, NAT_GROUP_W), lambda b, g, r: (0, b * nrb + r, g)),
            pl.BlockSpec((1, seq, NAT_GROUP_W), lambda b, g, r: (1, b, g)),
            pl.BlockSpec((1, seq, NAT_GROUP_W), lambda b, g, r: (2, b, g)),
            pl.BlockSpec((tq, NAT_GROUP_W), lambda b, g, r: (b * nrb + r, g)),
            pl.BlockSpec((NAT_KH, ppg, 2 * GRID_W, kh * GRID_W), lambda b, g, r: (0, g, 0, 0)),
        ],
        out_specs=pl.BlockSpec((tq, NAT_GROUP_W), lambda b, g, r: (b * nrb + r, g)),
        out_shape=jax.ShapeDtypeStruct((bt, NAT_W), BF16),
        compiler_params=pltpu.CompilerParams(
            dimension_semantics=("arbitrary", "arbitrary", "arbitrary"), vmem_limit_bytes=VMEM_LIMIT),
        name="nat2d",
    )(qkv, qkv, qkv, gn, bias)


def _nat_bias_table(rpb, rows):
    kh = min(NAT_KH, rows)
    cols = np.arange(GRID_W)
    col_start = np.clip(cols - NAT_KW // 2, 0, GRID_W - NAT_KW)
    rel = cols[None, :] - cols[:, None]
    valid = (cols[None, :] >= col_start[:, None]) & (cols[None, :] < col_start[:, None] + NAT_KW)
    cidx = np.clip(rel + NAT_KW - 1, 0, 2 * NAT_KW - 2)
    ridx = np.minimum(np.arange(NAT_KH)[:, None] + np.arange(kh)[None, :], 2 * NAT_KH - 2)
    t = rpb[:, ridx][:, :, :, cidx]
    t = jnp.where(valid[None, None, None], t.astype(F32), NEG)
    t = jnp.transpose(t, (1, 0, 3, 2, 4))
    return t.reshape(NAT_KH, rpb.shape[0] // 2, 2 * GRID_W, kh * GRID_W)


def _out_kernel(yf_ref, yr_ref, bonus_ref, grw_ref, yb_ref, x_ref, p_ref, lnw_ref, lnb_ref,
                wout_ref, pg_ref, wgate_ref, wproj_ref, fg_ref, o_ref):
    r1 = lax.broadcasted_iota(jnp.int32, (LANES, LANES), 0)
    c1 = lax.broadcasted_iota(jnp.int32, (LANES, LANES), 1)
    avg_bd = jnp.where((r1 // HEAD) == (c1 // HEAD), 1.0 / HEAD, 0.0).astype(BF16)
    parts = []
    for p in range(PAIRS):
        y = yf_ref[p] + yr_ref[p]
        mu = _dot(y.astype(BF16), avg_bd)
        dlt = y - mu
        var = _dot((dlt * dlt).astype(BF16), avg_bd)
        yn = dlt * lax.rsqrt(var + LNX_EPS) * lnw_ref[p] + lnb_ref[p]
        g = grw_ref[p]
        parts.append(((yn + bonus_ref[p]) * (g * _sigmoid(g))).astype(BF16))
    parts.append(yb_ref[...])
    mix = jnp.concatenate(parts, axis=1)
    h = x_ref[...] + _dot(mix, wout_ref[...])

    def rms(v, gain):
        ms = jnp.mean(v * v, axis=-1, keepdims=True)
        return v * lax.rsqrt(ms + NORM_EPS) * gain

    gate = _sigmoid(_dot(rms(h, pg_ref[...]).astype(BF16), wgate_ref[...]))
    h = h + _dot(p_ref[...].astype(BF16), wproj_ref[...]) * gate
    o_ref[...] = rms(h, fg_ref[...])


def _out_stage(yf, yr, bonus, grw, yb, x2, p2, lnw, lnb, w_out, ple_g, w_gate, w_proj, final_g, *, tm):
    bt, d = x2.shape
    ple = p2.shape[1]
    const = lambda shape: pl.BlockSpec(shape, lambda i: (0,) * len(shape), pipeline_mode=pl.Buffered(1))
    pair_spec = pl.BlockSpec((PAIRS, tm, LANES), lambda i: (0, i, 0))
    return pl.pallas_call(
        _out_kernel,
        grid=(bt // tm,),
        in_specs=[
            pair_spec, pair_spec, pair_spec, pair_spec,
            pl.BlockSpec((tm, NAT_W), lambda i: (i, 0)),
            pl.BlockSpec((tm, d), lambda i: (i, 0)),
            pl.BlockSpec((tm, ple), lambda i: (i, 0)),
            const((PAIRS, 1, LANES)), const((PAIRS, 1, LANES)),
            const((RWKV_W + NAT_W, d)), const((1, d)), const((d, d)), const((ple, d)), const((1, d)),
        ],
        out_specs=pl.BlockSpec((tm, d), lambda i: (i, 0)),
        out_shape=jax.ShapeDtypeStruct((bt, d), F32),
        compiler_params=pltpu.CompilerParams(
            dimension_semantics=("arbitrary",), vmem_limit_bytes=VMEM_LIMIT),
        name="out_stage",
    )(yf, yr, bonus, grw, yb, x2, p2, lnw, lnb, w_out, ple_g, w_gate, w_proj, final_g)


INPROJ_TM = 512
OUT_TM = 256


def kernel(x, p, norm_mix_g, w_in, shift_mu_prev, shift_mu_next, decay_w0, decay_w2, iclr_a0, iclr_a2,
           k_k, k_a, r_k, lnx_w, lnx_b, nat_rpb, w_out, ple_norm_g, w_ple_gate, w_ple_proj, final_norm_g):
    batch, seq, d = x.shape
    depth = p.shape[0]
    bt = batch * seq
    rows = seq // GRID_W
    o_wd = 3 * RWKV_W
    shift_w = o_wd + 4 * LORA
    o_nat = shift_w + RWKV_W
    pairs = lambda a: a.reshape(PAIRS, 1, LANES).astype(F32)

    assert depth == 1, "the output stage fuses the final norm: one trunk layer only"
    h = x.reshape(bt, d)
    w = w_in[0]
    w_main = jnp.concatenate([w[:, :o_wd], w[:, shift_w:]], axis=1).astype(BF16)
    w_lora = w[:, o_wd:shift_w].astype(BF16)
    mup, mun = shift_mu_prev[0], shift_mu_next[0]
    zrkv, grw, qkv, gn, lora = _inproj(
        h, norm_mix_g[0].reshape(1, d), w_main, w_lora,
        mup[:o_wd].reshape(3, 1, RWKV_W), mun[:o_wd].reshape(3, 1, RWKV_W),
        mup[o_wd:].reshape(1, 4 * LORA), mun[o_wd:].reshape(1, 4 * LORA),
        seq=seq, tm=INPROJ_TM)
    yf, yr, bonus = _wkv(
        zrkv, lora,
        decay_w0[0].reshape(2, 1, RWKV_W), decay_w2[0].astype(BF16),
        iclr_a0[0].reshape(2, 1, RWKV_W), iclr_a2[0].astype(BF16),
        pairs(k_k[0]), pairs(k_a[0]), pairs(r_k[0]), batch=batch, seq=seq)
    yb = _nat(qkv, gn, _nat_bias_table(nat_rpb[0], rows), batch=batch, seq=seq)
    out = _out_stage(
        yf, yr, bonus, grw, yb, h, p[0].reshape(bt, -1), pairs(lnx_w[0]), pairs(lnx_b[0]),
        w_out[0].astype(BF16), ple_norm_g[0].reshape(1, d), w_ple_gate[0].astype(BF16),
        w_ple_proj[0].astype(BF16), final_norm_g.reshape(1, d), tm=OUT_TM)
    return out.reshape(batch, seq, d)
```

```python
import functools
import math

import numpy as np
import jax
import jax.numpy as jnp
from jax import lax
from jax.experimental import pallas as pl
from jax.experimental.pallas import tpu as pltpu

GRID_W = 64
HEAD = 64
RWKV_W = 1024
NAT_W = 1024
LORA = 64
NAT_KH = 8
NAT_KW = 16
NORM_EPS = 1e-6
LNX_EPS = 64e-5
DECAY_SCALE = math.exp(-0.5)

LANES = 128
PACK_ROWS = 16
VMEM_LIMIT = 56 * 1024 * 1024

CHUNK = 64
SUB = CHUNK // 2
SUB_LOG2 = SUB.bit_length() - 1
assert SUB == 1 << SUB_LOG2
PAIRS = RWKV_W // LANES
NEG = -1e30

F32 = jnp.float32
BF16 = jnp.bfloat16


def _dot(a, b):
    return jnp.dot(a, b, preferred_element_type=F32)


def _dot_nt(a, b):
    return lax.dot_general(a, b, (((1,), (1,)), ((), ())), preferred_element_type=F32)


def _dot_tn(a, b):
    return lax.dot_general(a, b, (((0,), (0,)), ((), ())), preferred_element_type=F32)


def _sigmoid(x):
    return 1.0 / (1.0 + jnp.exp(-x))


def _round_robin(chains):
    chains = list(chains)
    while chains:
        alive = []
        for c in chains:
            try:
                next(c)
                alive.append(c)
            except StopIteration:
                pass
        chains = alive


def _inproj_kernel(x_ref, xp_ref, xn_ref, g_ref, w_ref, wl_ref, mup_ref, mun_ref, mupl_ref, munl_ref,
                   zrkv_ref, grw_ref, qkv_ref, gn_ref, lora_ref, hn_ref, *, tm, tiles_per_batch):
    i = pl.program_id(0)
    j = pl.program_id(1)
    halo = PACK_ROWS
    rows = tm + 2 * halo

    def shifted(z, mup, mun):
        zp = pltpu.roll(z, 1, 0)[halo:halo + tm]
        zn = pltpu.roll(z, rows - 1, 0)[halo:halo + tm]
        zc = z[halo:halo + tm]
        return zc + mup * (zp - zc) + mun * (zn - zc)

    @pl.when(j == 0)
    def _():
        g = g_ref[...]

        def norm(xv):
            ms = jnp.mean(xv * xv, axis=-1, keepdims=True)
            return xv * lax.rsqrt(ms + NORM_EPS) * g

        ti = i % tiles_per_batch
        keep_p = (ti > 0).astype(F32)
        keep_n = (ti < tiles_per_batch - 1).astype(F32)
        hn_ref[0:halo] = (norm(xp_ref[...]) * keep_p).astype(BF16)
        hn_ref[halo:halo + tm] = norm(x_ref[...]).astype(BF16)
        hn_ref[halo + tm:rows] = (norm(xn_ref[...]) * keep_n).astype(BF16)
        zl = _dot(hn_ref[...], wl_ref[...])
        lora_ref[...] = shifted(zl, mupl_ref[...], munl_ref[...])

    @pl.when(j < 3)
    def _():
        z = _dot(hn_ref[...], w_ref[...])
        zs = shifted(z, mup_ref[0], mun_ref[0])
        for p in range(PAIRS):
            zrkv_ref[0, p] = zs[:, p * LANES:(p + 1) * LANES]

    @pl.when(j == 3)
    def _():
        z = _dot(hn_ref[halo:halo + tm], w_ref[...])
        for p in range(PAIRS):
            grw_ref[p] = z[:, p * LANES:(p + 1) * LANES]

    @pl.when((j >= 4) & (j < 7))
    def _():
        qkv_ref[0] = _dot(hn_ref[halo:halo + tm], w_ref[...]).astype(BF16)

    @pl.when(j == 7)
    def _():
        gn_ref[...] = _dot(hn_ref[halo:halo + tm], w_ref[...])


def _inproj(x2, norm_g, w_main, w_lora, mup, mun, mupl, munl, *, seq, tm):
    bt, d = x2.shape
    halo = PACK_ROWS
    tiles_per_batch = seq // tm
    n_i = bt // tm
    hb = tm // halo
    last_hb = bt // halo - 1
    kernel = functools.partial(_inproj_kernel, tm=tm, tiles_per_batch=tiles_per_batch)
    return pl.pallas_call(
        kernel,
        grid=(n_i, 8),
        in_specs=[
            pl.BlockSpec((tm, d), lambda i, j: (i, 0)),
            pl.BlockSpec((halo, d), lambda i, j: (jnp.maximum(i * hb - 1, 0), 0)),
            pl.BlockSpec((halo, d), lambda i, j: (jnp.minimum((i + 1) * hb, last_hb), 0)),
            pl.BlockSpec((1, d), lambda i, j: (0, 0)),
            pl.BlockSpec((d, RWKV_W), lambda i, j: (0, j)),
            pl.BlockSpec((d, 4 * LORA), lambda i, j: (0, 0)),
            pl.BlockSpec((1, 1, RWKV_W), lambda i, j: (jnp.minimum(j, 2), 0, 0)),
            pl.BlockSpec((1, 1, RWKV_W), lambda i, j: (jnp.minimum(j, 2), 0, 0)),
            pl.BlockSpec((1, 4 * LORA), lambda i, j: (0, 0)),
            pl.BlockSpec((1, 4 * LORA), lambda i, j: (0, 0)),
        ],
        out_specs=[
            pl.BlockSpec((1, PAIRS, tm, LANES), lambda i, j: (jnp.minimum(j, 2), 0, i, 0)),
            pl.BlockSpec((PAIRS, tm, LANES), lambda i, j: (0, i, 0)),
            pl.BlockSpec((1, tm, NAT_W), lambda i, j: (jnp.clip(j - 4, 0, 2), i, 0)),
            pl.BlockSpec((tm, NAT_W), lambda i, j: (i, 0)),
            pl.BlockSpec((tm, 4 * LORA), lambda i, j: (i, 0)),
        ],
        out_shape=[
            jax.ShapeDtypeStruct((3, PAIRS, bt, LANES), F32),
            jax.ShapeDtypeStruct((PAIRS, bt, LANES), F32),
            jax.ShapeDtypeStruct((3, bt, NAT_W), BF16),
            jax.ShapeDtypeStruct((bt, NAT_W), F32),
            jax.ShapeDtypeStruct((bt, 4 * LORA), F32),
        ],
        scratch_shapes=[pltpu.VMEM((tm + 2 * halo, d), BF16)],
        compiler_params=pltpu.CompilerParams(
            dimension_semantics=("arbitrary", "arbitrary"), vmem_limit_bytes=VMEM_LIMIT),
        name="inproj",
    )(x2, x2, x2, norm_g, w_main, w_lora, mup, mun, mupl, munl)


def _wkv_kernel(zf_ref, zr_ref, lf_ref, lr_ref, w0_ref, w2_ref, a0_ref, a2_ref, kk_ref, ka_ref, rk_ref,
                yf_ref, yr_ref, bonus_ref, st_ref, cinc_ref, lw_ref, iclr_ref):
    L = CHUNK
    W4 = 2 * LANES
    nb = lf_ref.shape[0]

    @pl.when(pl.program_id(0) == 0)
    def _():
        st_ref[...] = jnp.zeros_like(st_ref)

    row = lax.broadcasted_iota(jnp.int32, (nb * L, nb * L), 0)
    col = lax.broadcasted_iota(jnp.int32, (nb * L, nb * L), 1)
    same = (row // L) == (col // L)
    for d, l_ref in ((0, lf_ref), (1, lr_ref)):
        lo = l_ref[...].reshape(nb * L, 4 * LORA)
        wd = lo[:, d * LORA:(d + 1) * LORA]
        ad = lo[:, (2 + d) * LORA:(3 + d) * LORA]
        dw = _dot(jnp.tanh(wd).astype(BF16), w2_ref[d])
        lw = -DECAY_SCALE * _sigmoid(w0_ref[d] + dw)
        da = _dot(ad.astype(BF16), a2_ref[d])
        ic = _sigmoid(a0_ref[d] + da)
        tri = (same & ((row >= col) if d == 0 else (col >= row))).astype(BF16)
        hi = lw.astype(BF16)
        lo2 = (lw - hi.astype(F32)).astype(BF16)
        cinc = _dot(tri, hi) + _dot(tri, lo2)
        for b in range(nb):
            for p in range(PAIRS):
                sl = slice(p * LANES, (p + 1) * LANES)
                cinc_ref[d, b, p] = cinc[b * L:(b + 1) * L, sl]
                lw_ref[d, b, p] = lw[b * L:(b + 1) * L, sl]
                iclr_ref[d, b, p] = ic[b * L:(b + 1) * L, sl]

    ri = lax.broadcasted_iota(jnp.int32, (L, W4), 0)
    li = lax.broadcasted_iota(jnp.int32, (L, W4), 1)
    blk = li // HEAD
    jj = li % HEAD
    fwd = li < LANES
    strict = (fwd & (jj < ri)) | (~fwd & (jj > ri))
    incl = (fwd & (jj <= ri)) | (~fwd & (jj >= ri))
    eye = (jj == ri).astype(F32)
    diag_blk = (jj // SUB) == (ri // SUB)
    r4 = lax.broadcasted_iota(jnp.int32, (W4, W4), 0)
    c4 = lax.broadcasted_iota(jnp.int32, (W4, W4), 1)
    bd_mask = (r4 // HEAD) == (c4 // HEAD)
    r1 = lax.broadcasted_iota(jnp.int32, (LANES, LANES), 0)
    c1 = lax.broadcasted_iota(jnp.int32, (LANES, LANES), 1)
    ones_bd = ((r1 // HEAD) == (c1 // HEAD)).astype(BF16)

    def bd(x):
        xb = x.astype(BF16)
        z = jnp.zeros_like(xb)
        return jnp.concatenate([jnp.where(blk == q, xb, z) for q in range(4)], axis=0)

    def mm(a, b):
        return _dot(a.astype(BF16), bd(b))

    def pair_chain(b, p):
        kkp = kk_ref[p]
        kap = ka_ref[p]
        rkp = rk_ref[p]
        sums = []
        for d, z_ref in ((0, zf_ref), (1, zr_ref)):
            kraw = z_ref[1, p, b] * kkp
            sums.append(_dot((kraw * kraw).astype(BF16), ones_bd))
        rk = _dot((zf_ref[0, p, b] * zf_ref[1, p, b] * rkp).astype(BF16), ones_bd)
        yield
        bonus_ref[p, b] = rk * zf_ref[2, p, b]
        rt, at, bt, kt, bh, kh, vv, pl_tot = [], [], [], [], [], [], [], []
        for d, z_ref in ((0, zf_ref), (1, zr_ref)):
            r = z_ref[0, p, b]
            k = z_ref[1, p, b]
            ci = cinc_ref[d, b, p]
            lw = lw_ref[d, b, p]
            ic = iclr_ref[d, b, p]
            kk = (k * kkp) * lax.rsqrt(jnp.maximum(sums[d], 1e-24))
            kd = k * (1.0 + (ic - 1.0) * kap)
            kb = kk * ic
            ctot = ci[L - 1:L] if d == 0 else ci[0:1]
            pinv = jnp.exp(-ci)
            phat = jnp.exp(ctot - ci)
            rt.append((r * jnp.exp(ci)).astype(BF16))
            at.append((-kk * jnp.exp(ci - lw)).astype(BF16))
            bt.append((kb * pinv).astype(BF16))
            kt.append((kd * pinv).astype(BF16))
            bh.append((kb * phat).astype(BF16))
            kh.append((kd * phat).astype(BF16))
            vv.append(z_ref[2, p, b].astype(BF16))
            pl_tot.append(jnp.exp(ctot))

        cat = lambda xs: jnp.concatenate(xs, axis=1)
        rt, at, bt, kt, bh, kh, vv, pl_tot = map(cat, (rt, at, bt, kt, bh, kh, vv, pl_tot))
        bhkh = jnp.concatenate([bh, kh], axis=0)

        o1 = _dot_nt(jnp.concatenate([at, rt], axis=0),
                     jnp.concatenate([bd(bt), bd(kt)], axis=0))
        yield
        a_ab = jnp.where(strict, o1[0:L, 0:W4], 0.0)
        a_ak = jnp.where(strict, o1[0:L, W4:2 * W4], 0.0)
        a_r = jnp.concatenate([jnp.where(incl, o1[L:2 * L, 0:W4], 0.0),
                               jnp.where(incl, o1[L:2 * L, W4:2 * W4], 0.0)], axis=1).astype(BF16)

        dg = jnp.where(diag_blk, a_ab, 0.0)
        off = (a_ab - dg).astype(BF16)
        tn = eye + dg
        pw = mm(dg, dg)
        av = mm(a_ak, vv)
        yield
        for _ in range(SUB_LOG2 - 2):
            o = mm(jnp.concatenate([tn, pw], axis=0), pw)
            yield
            tn = tn + o[0:L]
            pw = o[L:2 * L]
        o = mm(tn, pw)
        yield
        td = tn + o
        f = mm(td, off)
        o5 = _dot_nt(jnp.concatenate([at, rt], axis=0), st_ref[b, p].astype(BF16))
        yield
        o = mm(f, td)
        yield
        tinv = td + o
        o = mm(tinv, o5[0:L] + av)
        yield
        u = o.astype(BF16)
        o6 = _dot(a_r, jnp.concatenate([bd(u), bd(vv)], axis=0))
        g_full = _dot_tn(jnp.concatenate([u, vv], axis=0), bhkh)
        yield
        y = o5[L:2 * L] + o6
        st_ref[b, p] = st_ref[b, p] * pl_tot + jnp.where(bd_mask, g_full, 0.0)
        yf_ref[p, b] = y[:, 0:LANES]
        yr_ref[p, b] = y[:, LANES:2 * LANES]

    _round_robin([pair_chain(b, p) for b in range(nb) for p in range(PAIRS)])


def _wkv(zrkv, lora, w0, w2, a0, a2, k_k, k_a, r_k, *, batch, seq):
    nc = seq // CHUNK
    z5 = zrkv.reshape(3, PAIRS, batch, seq, LANES)
    lora3 = lora.reshape(batch, seq, 4 * LORA)
    full = lambda shape: pl.BlockSpec(shape, lambda i: (0,) * len(shape))
    z_blk = (3, PAIRS, batch, CHUNK, LANES)
    y_blk = (PAIRS, batch, CHUNK, LANES)
    scr = (2, batch, PAIRS, CHUNK, LANES)
    yf, yr, bonus = pl.pallas_call(
        _wkv_kernel,
        grid=(nc,),
        in_specs=[
            pl.BlockSpec(z_blk, lambda i: (0, 0, 0, i, 0)),
            pl.BlockSpec(z_blk, lambda i: (0, 0, 0, nc - 1 - i, 0)),
            pl.BlockSpec((batch, CHUNK, 4 * LORA), lambda i: (0, i, 0)),
            pl.BlockSpec((batch, CHUNK, 4 * LORA), lambda i: (0, nc - 1 - i, 0)),
            full((2, 1, RWKV_W)), full((2, LORA, RWKV_W)),
            full((2, 1, RWKV_W)), full((2, LORA, RWKV_W)),
            full((PAIRS, 1, LANES)), full((PAIRS, 1, LANES)), full((PAIRS, 1, LANES)),
        ],
        out_specs=[
            pl.BlockSpec(y_blk, lambda i: (0, 0, i, 0)),
            pl.BlockSpec(y_blk, lambda i: (0, 0, nc - 1 - i, 0)),
            pl.BlockSpec(y_blk, lambda i: (0, 0, i, 0)),
        ],
        out_shape=[jax.ShapeDtypeStruct((PAIRS, batch, seq, LANES), F32)] * 3,
        scratch_shapes=[
            pltpu.VMEM((batch, PAIRS, 2 * LANES, 2 * LANES), F32),
            pltpu.VMEM(scr, F32),
            pltpu.VMEM(scr, F32),
            pltpu.VMEM(scr, F32),
        ],
        compiler_params=pltpu.CompilerParams(
            dimension_semantics=("arbitrary",), vmem_limit_bytes=VMEM_LIMIT),
        name="wkv7",
    )(z5, z5, lora3, lora3, w0, w2, a0, a2, k_k, k_a, r_k)
    flat = lambda a: a.reshape(PAIRS, batch * seq, LANES)
    return flat(yf), flat(yr), flat(bonus)


NAT_ROWS_PER_STEP = 8
NAT_GROUP_W = 2 * LANES


def _nat_kernel(q_ref, k_ref, v_ref, g_ref, bias_ref, o_ref, *, rows):
    rb = pl.program_id(2)
    kh = min(NAT_KH, rows)
    win = kh * GRID_W
    scale = HEAD ** -0.5
    assert math.frexp(scale)[0] == 0.5, "scale is folded into bf16 q: must be a power of two"
    lane = lax.broadcasted_iota(jnp.int32, (GRID_W, LANES), 1)
    low = lane < HEAD

    def chain(qr, pp):
        r = rb * NAT_ROWS_PER_STEP + qr
        rs = jnp.clip(r - kh // 2, 0, rows - kh)
        pidx = rs - r + (NAT_KH - 1)
        q0 = qr * GRID_W
        k0 = pl.multiple_of(rs * GRID_W, GRID_W)
        sl = slice(pp * LANES, (pp + 1) * LANES)
        qp = q_ref[0, q0:q0 + GRID_W, sl]
        zq = jnp.zeros_like(qp)
        qs = jnp.concatenate([jnp.where(low, qp, zq), jnp.where(low, zq, qp)], axis=0)
        s = _dot_nt(qs, k_ref[0, pl.ds(k0, win), sl])
        yield
        s = s * scale + bias_ref[pidx, pp]
        m = jnp.max(s, axis=-1, keepdims=True)
        e = jnp.exp(s - m)
        l = jnp.sum(e, axis=-1, keepdims=True)
        o2 = _dot(e.astype(BF16), v_ref[0, pl.ds(k0, win), sl])
        yield
        o2 = o2 / l
        o = jnp.where(low, o2[0:GRID_W], o2[GRID_W:2 * GRID_W])
        g = g_ref[q0:q0 + GRID_W, sl]
        o_ref[q0:q0 + GRID_W, sl] = (o * (g * _sigmoid(g))).astype(o_ref.dtype)

    _round_robin([chain(qr, pp) for qr in range(NAT_ROWS_PER_STEP)
                  for pp in range(NAT_GROUP_W // LANES)])


def _nat(qkv, gn, bias, *, batch, seq):
    bt = gn.shape[0]
    rows = seq // GRID_W
    kh = min(NAT_KH, rows)
    tq = NAT_ROWS_PER_STEP * GRID_W
    nrb = rows // NAT_ROWS_PER_STEP
    ngrp = NAT_W // NAT_GROUP_W
    ppg = NAT_GROUP_W // LANES
    kernel = functools.partial(_nat_kernel, rows=rows)
    return pl.pallas_call(
        kernel,
        grid=(batch, ngrp, nrb),
        in_specs=[
            pl.BlockSpec((1, tq, NAT_GROUP_W), lambda b, g, r: (0, b * nrb + r, g)),
            pl.BlockSpec((1, seq, NAT_GROUP_W), lambda b, g, r: (1, b, g)),
            pl.BlockSpec((1, seq, NAT_GROUP_W), lambda b, g, r: (2, b, g)),
            pl.BlockSpec((tq, NAT_GROUP_W), lambda b, g, r: (b * nrb + r, g)),
            pl.BlockSpec((NAT_KH, ppg, 2 * GRID_W, kh * GRID_W), lambda b, g, r: (0, g, 0, 0)),
        ],
        out_specs=pl.BlockSpec((tq, NAT_GROUP_W), lambda b, g, r: (b * nrb + r, g)),
        out_shape=jax.ShapeDtypeStruct((bt, NAT_W), BF16),
        compiler_params=pltpu.CompilerParams(
            dimension_semantics=("arbitrary", "arbitrary", "arbitrary"), vmem_limit_bytes=VMEM_LIMIT),
        name="nat2d",
    )(qkv, qkv, qkv, gn, bias)


def _nat_bias_table(rpb, rows):
    kh = min(NAT_KH, rows)
    heads = rpb.shape[0]
    cols = np.arange(GRID_W)
    col_start = np.clip(cols - NAT_KW // 2, 0, GRID_W - NAT_KW)
    valid = (cols[None, :] >= col_start[:, None]) & (cols[None, :] < col_start[:, None] + NAT_KW)
    cidx = np.clip(cols[None, :] - cols[:, None] + NAT_KW - 1, 0, 2 * NAT_KW - 2)
    t = jnp.transpose(rpb.astype(F32), (0, 2, 1))[:, cidx]
    t = jnp.where(valid[None, :, :, None], t, NEG)
    t = jnp.transpose(t, (0, 1, 3, 2))
    t = jnp.stack([t[:, :, pi:pi + kh] for pi in range(NAT_KH)])
    return t.reshape(NAT_KH, heads // 2, 2 * GRID_W, kh * GRID_W)


def _out_kernel(yf_ref, yr_ref, bonus_ref, grw_ref, yb_ref, x_ref, p_ref, lnw_ref, lnb_ref,
                wout_ref, pg_ref, wgate_ref, wproj_ref, fg_ref, o_ref):
    r1 = lax.broadcasted_iota(jnp.int32, (LANES, LANES), 0)
    c1 = lax.broadcasted_iota(jnp.int32, (LANES, LANES), 1)
    avg_bd = jnp.where((r1 // HEAD) == (c1 // HEAD), 1.0 / HEAD, 0.0).astype(BF16)
    parts = []
    for p in range(PAIRS):
        y = yf_ref[p] + yr_ref[p]
        mu = _dot(y.astype(BF16), avg_bd)
        dlt = y - mu
        var = _dot((dlt * dlt).astype(BF16), avg_bd)
        yn = dlt * lax.rsqrt(var + LNX_EPS) * lnw_ref[p] + lnb_ref[p]
        g = grw_ref[p]
        parts.append(((yn + bonus_ref[p]) * (g * _sigmoid(g))).astype(BF16))
    parts.append(yb_ref[...])
    mix = jnp.concatenate(parts, axis=1)
    h = x_ref[...] + _dot(mix, wout_ref[...])

    def rms(v, gain):
        ms = jnp.mean(v * v, axis=-1, keepdims=True)
        return v * lax.rsqrt(ms + NORM_EPS) * gain

    gate = _sigmoid(_dot(rms(h, pg_ref[...]).astype(BF16), wgate_ref[...]))
    h = h + _dot(p_ref[...].astype(BF16), wproj_ref[...]) * gate
    o_ref[...] = rms(h, fg_ref[...])


def _out_stage(yf, yr, bonus, grw, yb, x2, p2, lnw, lnb, w_out, ple_g, w_gate, w_proj, final_g, *, tm):
    bt, d = x2.shape
    ple = p2.shape[1]
    const = lambda shape: pl.BlockSpec(shape, lambda i: (0,) * len(shape), pipeline_mode=pl.Buffered(1))
    pair_spec = pl.BlockSpec((PAIRS, tm, LANES), lambda i: (0, i, 0))
    return pl.pallas_call(
        _out_kernel,
        grid=(bt // tm,),
        in_specs=[
            pair_spec, pair_spec, pair_spec, pair_spec,
            pl.BlockSpec((tm, NAT_W), lambda i: (i, 0)),
            pl.BlockSpec((tm, d), lambda i: (i, 0)),
            pl.BlockSpec((tm, ple), lambda i: (i, 0)),
            const((PAIRS, 1, LANES)), const((PAIRS, 1, LANES)),
            const((RWKV_W + NAT_W, d)), const((1, d)), const((d, d)), const((ple, d)), const((1, d)),
        ],
        out_specs=pl.BlockSpec((tm, d), lambda i: (i, 0)),
        out_shape=jax.ShapeDtypeStruct((bt, d), F32),
        compiler_params=pltpu.CompilerParams(
            dimension_semantics=("arbitrary",), vmem_limit_bytes=VMEM_LIMIT),
        name="out_stage",
    )(yf, yr, bonus, grw, yb, x2, p2, lnw, lnb, w_out, ple_g, w_gate, w_proj, final_g)


INPROJ_TM = 512
OUT_TM = 256


def kernel(x, p, norm_mix_g, w_in, shift_mu_prev, shift_mu_next, decay_w0, decay_w2, iclr_a0, iclr_a2,
           k_k, k_a, r_k, lnx_w, lnx_b, nat_rpb, w_out, ple_norm_g, w_ple_gate, w_ple_proj, final_norm_g):
    batch, seq, d = x.shape
    depth = p.shape[0]
    bt = batch * seq
    rows = seq // GRID_W
    o_wd = 3 * RWKV_W
    shift_w = o_wd + 4 * LORA
    pairs = lambda a: a.reshape(PAIRS, 1, LANES).astype(F32)

    assert depth == 1, "the output stage fuses the final norm: one trunk layer only"
    h = x.reshape(bt, d)
    w = w_in[0]
    w_main = jnp.concatenate([w[:, :o_wd], w[:, shift_w:]], axis=1).astype(BF16)
    w_lora = w[:, o_wd:shift_w].astype(BF16)
    mup, mun = shift_mu_prev[0], shift_mu_next[0]
    zrkv, grw, qkv, gn, lora = _inproj(
        h, norm_mix_g[0].reshape(1, d), w_main, w_lora,
        mup[:o_wd].reshape(3, 1, RWKV_W), mun[:o_wd].reshape(3, 1, RWKV_W),
        mup[o_wd:].reshape(1, 4 * LORA), mun[o_wd:].reshape(1, 4 * LORA),
        seq=seq, tm=INPROJ_TM)
    yf, yr, bonus = _wkv(
        zrkv, lora,
        decay_w0[0].reshape(2, 1, RWKV_W), decay_w2[0].astype(BF16),
        iclr_a0[0].reshape(2, 1, RWKV_W), iclr_a2[0].astype(BF16),
        pairs(k_k[0]), pairs(k_a[0]), pairs(r_k[0]), batch=batch, seq=seq)
    yb = _nat(qkv, gn, _nat_bias_table(nat_rpb[0], rows), batch=batch, seq=seq)
    out = _out_stage(
        yf, yr, bonus, grw, yb, h, p[0].reshape(bt, -1), pairs(lnx_w[0]), pairs(lnx_b[0]),
        w_out[0].astype(BF16), ple_norm_g[0].reshape(1, d), w_ple_gate[0].astype(BF16),
        w_ple_proj[0].astype(BF16), final_norm_g.reshape(1, d), tm=OUT_TM)
    return out.reshape(batch, seq, d)
```

```python
import functools
import math

import numpy as np
import jax
import jax.numpy as jnp
from jax import lax
from jax.experimental import pallas as pl
from jax.experimental.pallas import tpu as pltpu

GRID_W = 64
HEAD = 64
RWKV_W = 1024
NAT_W = 1024
LORA = 64
NAT_KH = 8
NAT_KW = 16
NORM_EPS = 1e-6
LNX_EPS = 64e-5
DECAY_SCALE = math.exp(-0.5)

LANES = 128
PACK_ROWS = 16
VMEM_LIMIT = 56 * 1024 * 1024

CHUNK = 64
SUB = CHUNK // 2
SUB_LOG2 = SUB.bit_length() - 1
assert SUB == 1 << SUB_LOG2
PAIRS = RWKV_W // LANES
NEG = -1e30

F32 = jnp.float32
BF16 = jnp.bfloat16


def _dot(a, b):
    return jnp.dot(a, b, preferred_element_type=F32)


def _dot_nt(a, b):
    return lax.dot_general(a, b, (((1,), (1,)), ((), ())), preferred_element_type=F32)


def _dot_tn(a, b):
    return lax.dot_general(a, b, (((0,), (0,)), ((), ())), preferred_element_type=F32)


def _sigmoid(x):
    return 1.0 / (1.0 + jnp.exp(-x))


def _round_robin(chains):
    chains = list(chains)
    while chains:
        alive = []
        for c in chains:
            try:
                next(c)
                alive.append(c)
            except StopIteration:
                pass
        chains = alive


def _inproj_kernel(x_ref, xp_ref, xn_ref, g_ref, ws_ref, w_ref, wl_ref, mup_ref, mun_ref, mupl_ref, munl_ref,
                   zrkv_ref, grw_ref, qkv_ref, gn_ref, lora_ref, hn_ref, *, tm, tiles_per_batch):
    i = pl.program_id(0)
    j = pl.program_id(1)
    halo = PACK_ROWS
    rows = tm + 2 * halo

    def shifted(z, mup, mun):
        zp = pltpu.roll(z, 1, 0)[halo:halo + tm]
        zn = pltpu.roll(z, rows - 1, 0)[halo:halo + tm]
        zc = z[halo:halo + tm]
        return zc + mup * (zp - zc) + mun * (zn - zc)

    @pl.when(j == 0)
    def _():
        g = g_ref[...]

        def norm(xv):
            ms = jnp.mean(xv * xv, axis=-1, keepdims=True)
            return xv * lax.rsqrt(ms + NORM_EPS) * g

        ti = i % tiles_per_batch
        keep_p = (ti > 0).astype(F32)
        keep_n = (ti < tiles_per_batch - 1).astype(F32)
        hn_ref[0:halo] = (norm(xp_ref[...]) * keep_p).astype(BF16)
        hn_ref[halo:halo + tm] = norm(x_ref[...]).astype(BF16)
        hn_ref[halo + tm:rows] = (norm(xn_ref[...]) * keep_n).astype(BF16)
        zl = _dot(hn_ref[...], wl_ref[...])
        lora_ref[...] = shifted(zl, mupl_ref[...], munl_ref[...])

    @pl.when(j < 3)
    def _():
        z = _dot(hn_ref[...], ws_ref[...])
        zs = shifted(z, mup_ref[0], mun_ref[0])
        for p in range(PAIRS):
            zrkv_ref[0, p] = zs[:, p * LANES:(p + 1) * LANES]

    @pl.when(j == 3)
    def _():
        z = _dot(hn_ref[halo:halo + tm], w_ref[...])
        for p in range(PAIRS):
            grw_ref[p] = z[:, p * LANES:(p + 1) * LANES]

    @pl.when((j >= 4) & (j < 7))
    def _():
        qkv_ref[0] = _dot(hn_ref[halo:halo + tm], w_ref[...]).astype(BF16)

    @pl.when(j == 7)
    def _():
        gn_ref[...] = _dot(hn_ref[halo:halo + tm], w_ref[...])


def _inproj(x2, norm_g, w_shift, w_rest, w_lora, mup, mun, mupl, munl, *, seq, tm):
    bt, d = x2.shape
    halo = PACK_ROWS
    tiles_per_batch = seq // tm
    n_i = bt // tm
    hb = tm // halo
    last_hb = bt // halo - 1
    kernel = functools.partial(_inproj_kernel, tm=tm, tiles_per_batch=tiles_per_batch)
    return pl.pallas_call(
        kernel,
        grid=(n_i, 8),
        in_specs=[
            pl.BlockSpec((tm, d), lambda i, j: (i, 0)),
            pl.BlockSpec((halo, d), lambda i, j: (jnp.maximum(i * hb - 1, 0), 0)),
            pl.BlockSpec((halo, d), lambda i, j: (jnp.minimum((i + 1) * hb, last_hb), 0)),
            pl.BlockSpec((1, d), lambda i, j: (0, 0)),
            pl.BlockSpec((d, RWKV_W), lambda i, j: (0, jnp.minimum(j, 2))),
            pl.BlockSpec((d, RWKV_W), lambda i, j: (0, jnp.maximum(j - 3, 0))),
            pl.BlockSpec((d, 4 * LORA), lambda i, j: (0, 0)),
            pl.BlockSpec((1, 1, RWKV_W), lambda i, j: (jnp.minimum(j, 2), 0, 0)),
            pl.BlockSpec((1, 1, RWKV_W), lambda i, j: (jnp.minimum(j, 2), 0, 0)),
            pl.BlockSpec((1, 4 * LORA), lambda i, j: (0, 0)),
            pl.BlockSpec((1, 4 * LORA), lambda i, j: (0, 0)),
        ],
        out_specs=[
            pl.BlockSpec((1, PAIRS, tm, LANES), lambda i, j: (jnp.minimum(j, 2), 0, i, 0)),
            pl.BlockSpec((PAIRS, tm, LANES), lambda i, j: (0, i, 0)),
            pl.BlockSpec((1, tm, NAT_W), lambda i, j: (jnp.clip(j - 4, 0, 2), i, 0)),
            pl.BlockSpec((tm, NAT_W), lambda i, j: (i, 0)),
            pl.BlockSpec((tm, 4 * LORA), lambda i, j: (i, 0)),
        ],
        out_shape=[
            jax.ShapeDtypeStruct((3, PAIRS, bt, LANES), F32),
            jax.ShapeDtypeStruct((PAIRS, bt, LANES), F32),
            jax.ShapeDtypeStruct((3, bt, NAT_W), BF16),
            jax.ShapeDtypeStruct((bt, NAT_W), F32),
            jax.ShapeDtypeStruct((bt, 4 * LORA), F32),
        ],
        scratch_shapes=[pltpu.VMEM((tm + 2 * halo, d), BF16)],
        compiler_params=pltpu.CompilerParams(
            dimension_semantics=("arbitrary", "arbitrary"), vmem_limit_bytes=VMEM_LIMIT),
        name="inproj",
    )(x2, x2, x2, norm_g, w_shift, w_rest, w_lora, mup, mun, mupl, munl)


def _wkv_kernel(zf_ref, zr_ref, lf_ref, lr_ref, w0_ref, w2_ref, a0_ref, a2_ref, kk_ref, ka_ref, rk_ref,
                yf_ref, yr_ref, bonus_ref, st_ref, cinc_ref, lw_ref, iclr_ref):
    L = CHUNK
    W4 = 2 * LANES
    nb = lf_ref.shape[0]

    @pl.when(pl.program_id(0) == 0)
    def _():
        st_ref[...] = jnp.zeros_like(st_ref)

    row = lax.broadcasted_iota(jnp.int32, (nb * L, nb * L), 0)
    col = lax.broadcasted_iota(jnp.int32, (nb * L, nb * L), 1)
    same = (row // L) == (col // L)
    for d, l_ref in ((0, lf_ref), (1, lr_ref)):
        lo = l_ref[...].reshape(nb * L, 4 * LORA)
        wd = lo[:, d * LORA:(d + 1) * LORA]
        ad = lo[:, (2 + d) * LORA:(3 + d) * LORA]
        dw = _dot(jnp.tanh(wd).astype(BF16), w2_ref[d])
        lw = -DECAY_SCALE * _sigmoid(w0_ref[d] + dw)
        da = _dot(ad.astype(BF16), a2_ref[d])
        ic = _sigmoid(a0_ref[d] + da)
        tri = (same & ((row >= col) if d == 0 else (col >= row))).astype(BF16)
        hi = lw.astype(BF16)
        lo2 = (lw - hi.astype(F32)).astype(BF16)
        cinc = _dot(tri, hi) + _dot(tri, lo2)
        for b in range(nb):
            for p in range(PAIRS):
                sl = slice(p * LANES, (p + 1) * LANES)
                cinc_ref[d, b, p] = cinc[b * L:(b + 1) * L, sl]
                lw_ref[d, b, p] = lw[b * L:(b + 1) * L, sl]
                iclr_ref[d, b, p] = ic[b * L:(b + 1) * L, sl]

    ri = lax.broadcasted_iota(jnp.int32, (L, W4), 0)
    li = lax.broadcasted_iota(jnp.int32, (L, W4), 1)
    blk = li // HEAD
    jj = li % HEAD
    fwd = li < LANES
    strict = (fwd & (jj < ri)) | (~fwd & (jj > ri))
    incl = (fwd & (jj <= ri)) | (~fwd & (jj >= ri))
    eye = (jj == ri).astype(F32)
    diag_blk = (jj // SUB) == (ri // SUB)
    r1 = lax.broadcasted_iota(jnp.int32, (LANES, LANES), 0)
    c1 = lax.broadcasted_iota(jnp.int32, (LANES, LANES), 1)
    ones_bd_mask = (r1 // HEAD) == (c1 // HEAD)
    ones_bd = ones_bd_mask.astype(BF16)

    def bd(x):
        xb = x.astype(BF16)
        z = jnp.zeros_like(xb)
        return jnp.concatenate([jnp.where(blk == q, xb, z) for q in range(4)], axis=0)

    def mm(a, b):
        return _dot(a.astype(BF16), bd(b))

    def pair_chain(b, p):
        kkp = kk_ref[p]
        kap = ka_ref[p]
        rkp = rk_ref[p]
        sums = []
        for d, z_ref in ((0, zf_ref), (1, zr_ref)):
            kraw = z_ref[1, p, b] * kkp
            sums.append(_dot((kraw * kraw).astype(BF16), ones_bd))
        rk = _dot((zf_ref[0, p, b] * zf_ref[1, p, b] * rkp).astype(BF16), ones_bd)
        yield
        bonus_ref[p, b] = rk * zf_ref[2, p, b]
        rt, at, bt, kt, bh, kh, vv, pl_tot = [], [], [], [], [], [], [], []
        for d, z_ref in ((0, zf_ref), (1, zr_ref)):
            r = z_ref[0, p, b]
            k = z_ref[1, p, b]
            ci = cinc_ref[d, b, p]
            lw = lw_ref[d, b, p]
            ic = iclr_ref[d, b, p]
            kk = (k * kkp) * lax.rsqrt(jnp.maximum(sums[d], 1e-24))
            kd = k * (1.0 + (ic - 1.0) * kap)
            kb = kk * ic
            ctot = ci[L - 1:L] if d == 0 else ci[0:1]
            pinv = jnp.exp(-ci)
            ptot = jnp.exp(ctot)
            bti = kb * pinv
            kti = kd * pinv
            rt.append((r * jnp.exp(ci)).astype(BF16))
            at.append((-kk * jnp.exp(ci - lw)).astype(BF16))
            bt.append(bti.astype(BF16))
            kt.append(kti.astype(BF16))
            bh.append((bti * ptot).astype(BF16))
            kh.append((kti * ptot).astype(BF16))
            vv.append(z_ref[2, p, b].astype(BF16))
            pl_tot.append(ptot)

        cat = lambda xs: jnp.concatenate(xs, axis=1)
        rt, at, bt, kt, bh, kh, vv, pl_tot = map(cat, (rt, at, bt, kt, bh, kh, vv, pl_tot))
        bhkh = jnp.concatenate([bh, kh], axis=0)

        o1 = _dot(jnp.concatenate([at, rt], axis=0),
                  jnp.concatenate([bd(bt).T, bd(kt).T], axis=1))
        yield
        a_ab = jnp.where(strict, o1[0:L, 0:W4], 0.0)
        a_ak = jnp.where(strict, o1[0:L, W4:2 * W4], 0.0)
        a_r = jnp.concatenate([jnp.where(incl, o1[L:2 * L, 0:W4], 0.0),
                               jnp.where(incl, o1[L:2 * L, W4:2 * W4], 0.0)], axis=1).astype(BF16)

        dg = jnp.where(diag_blk, a_ab, 0.0)
        off = (a_ab - dg).astype(BF16)
        tn = eye + dg
        pw = mm(dg, dg)
        av = mm(a_ak, vv)
        yield
        for _ in range(SUB_LOG2 - 2):
            o = mm(jnp.concatenate([tn, pw], axis=0), pw)
            yield
            tn = tn + o[0:L]
            pw = o[L:2 * L]
        o = mm(tn, pw)
        yield
        td = tn + o
        f = mm(td, off)
        ar = jnp.concatenate([at, rt], axis=0)
        o5 = jnp.concatenate(
            [_dot_nt(ar[:, d * LANES:(d + 1) * LANES], st_ref[b, p, d].astype(BF16))
             for d in range(2)], axis=1)
        yield
        o = mm(f, td)
        yield
        tinv = td + o
        o = mm(tinv, o5[0:L] + av)
        yield
        u = o.astype(BF16)
        o6 = _dot(a_r, jnp.concatenate([bd(u), bd(vv)], axis=0))
        uvv = jnp.concatenate([u, vv], axis=0)
        gs = [_dot_tn(uvv[:, d * LANES:(d + 1) * LANES], bhkh[:, d * LANES:(d + 1) * LANES])
              for d in range(2)]
        yield
        y = o5[L:2 * L] + o6
        for d in range(2):
            st_ref[b, p, d] = (st_ref[b, p, d] * pl_tot[:, d * LANES:(d + 1) * LANES]
                               + jnp.where(ones_bd_mask, gs[d], 0.0))
        yf_ref[p, b] = y[:, 0:LANES]
        yr_ref[p, b] = y[:, LANES:2 * LANES]

    _round_robin([pair_chain(b, p) for b in range(nb) for p in range(PAIRS)])


def _wkv(zrkv, lora, w0, w2, a0, a2, k_k, k_a, r_k, *, batch, seq):
    nc = seq // CHUNK
    z5 = zrkv.reshape(3, PAIRS, batch, seq, LANES)
    lora3 = lora.reshape(batch, seq, 4 * LORA)
    full = lambda shape: pl.BlockSpec(shape, lambda i: (0,) * len(shape))
    z_blk = (3, PAIRS, batch, CHUNK, LANES)
    y_blk = (PAIRS, batch, CHUNK, LANES)
    scr = (2, batch, PAIRS, CHUNK, LANES)
    yf, yr, bonus = pl.pallas_call(
        _wkv_kernel,
        grid=(nc,),
        in_specs=[
            pl.BlockSpec(z_blk, lambda i: (0, 0, 0, i, 0)),
            pl.BlockSpec(z_blk, lambda i: (0, 0, 0, nc - 1 - i, 0)),
            pl.BlockSpec((batch, CHUNK, 4 * LORA), lambda i: (0, i, 0)),
            pl.BlockSpec((batch, CHUNK, 4 * LORA), lambda i: (0, nc - 1 - i, 0)),
            full((2, 1, RWKV_W)), full((2, LORA, RWKV_W)),
            full((2, 1, RWKV_W)), full((2, LORA, RWKV_W)),
            full((PAIRS, 1, LANES)), full((PAIRS, 1, LANES)), full((PAIRS, 1, LANES)),
        ],
        out_specs=[
            pl.BlockSpec(y_blk, lambda i: (0, 0, i, 0)),
            pl.BlockSpec(y_blk, lambda i: (0, 0, nc - 1 - i, 0)),
            pl.BlockSpec(y_blk, lambda i: (0, 0, i, 0)),
        ],
        out_shape=[jax.ShapeDtypeStruct((PAIRS, batch, seq, LANES), F32)] * 3,
        scratch_shapes=[
            pltpu.VMEM((batch, PAIRS, 2, LANES, LANES), F32),
            pltpu.VMEM(scr, F32),
            pltpu.VMEM(scr, F32),
            pltpu.VMEM(scr, F32),
        ],
        compiler_params=pltpu.CompilerParams(
            dimension_semantics=("arbitrary",), vmem_limit_bytes=VMEM_LIMIT),
        name="wkv7",
    )(z5, z5, lora3, lora3, w0, w2, a0, a2, k_k, k_a, r_k)
    flat = lambda a: a.reshape(PAIRS, batch * seq, LANES)
    return flat(yf), flat(yr), flat(bonus)


NAT_ROWS_PER_STEP = 8
NAT_GROUP_W = 2 * LANES


def _nat_kernel(q_ref, k_ref, v_ref, g_ref, bias_ref, o_ref, *, rows):
    rb = pl.program_id(2)
    kh = min(NAT_KH, rows)
    win = kh * GRID_W
    scale = HEAD ** -0.5
    assert math.frexp(scale)[0] == 0.5, "scale is folded into bf16 q: must be a power of two"
    lane = lax.broadcasted_iota(jnp.int32, (GRID_W, LANES), 1)
    low = lane < HEAD

    def chain(qr, pp):
        r = rb * NAT_ROWS_PER_STEP + qr
        rs = jnp.clip(r - kh // 2, 0, rows - kh)
        pidx = rs - r + (NAT_KH - 1)
        q0 = qr * GRID_W
        k0 = pl.multiple_of(rs * GRID_W, GRID_W)
        sl = slice(pp * LANES, (pp + 1) * LANES)
        qp = q_ref[0, q0:q0 + GRID_W, sl]
        zq = jnp.zeros_like(qp)
        qs = jnp.concatenate([jnp.where(low, qp, zq), jnp.where(low, zq, qp)], axis=0)
        s = _dot_nt(qs, k_ref[0, pl.ds(k0, win), sl])
        yield
        s = s * scale + bias_ref[pidx, pp]
        m = jnp.max(s, axis=-1, keepdims=True)
        e = jnp.exp(s - m)
        l = jnp.sum(e, axis=-1, keepdims=True)
        o2 = _dot(e.astype(BF16), v_ref[0, pl.ds(k0, win), sl])
        yield
        o2 = o2 / l
        o = jnp.where(low, o2[0:GRID_W], o2[GRID_W:2 * GRID_W])
        g = g_ref[q0:q0 + GRID_W, sl]
        o_ref[q0:q0 + GRID_W, sl] = (o * (g * _sigmoid(g))).astype(o_ref.dtype)

    _round_robin([chain(qr, pp) for qr in range(NAT_ROWS_PER_STEP)
                  for pp in range(NAT_GROUP_W // LANES)])


def _nat(qkv, gn, bias, *, batch, seq):
    bt = gn.shape[0]
    rows = seq // GRID_W
    kh = min(NAT_KH, rows)
    tq = NAT_ROWS_PER_STEP * GRID_W
    nrb = rows // NAT_ROWS_PER_STEP
    ngrp = NAT_W // NAT_GROUP_W
    ppg = NAT_GROUP_W // LANES
    kernel = functools.partial(_nat_kernel, rows=rows)
    return pl.pallas_call(
        kernel,
        grid=(batch, ngrp, nrb),
        in_specs=[
            pl.BlockSpec((1, tq, NAT_GROUP_W), lambda b, g, r: (0, b * nrb + r, g)),
            pl.BlockSpec((1, seq, NAT_GROUP_W), lambda b, g, r: (1, b, g)),
            pl.BlockSpec((1, seq, NAT_GROUP_W), lambda b, g, r: (2, b, g)),
            pl.BlockSpec((tq, NAT_GROUP_W), lambda b, g, r: (b * nrb + r, g)),
            pl.BlockSpec((NAT_KH, ppg, 2 * GRID_W, kh * GRID_W), lambda b, g, r: (0, g, 0, 0)),
        ],
        out_specs=pl.BlockSpec((tq, NAT_GROUP_W), lambda b, g, r: (b * nrb + r, g)),
        out_shape=jax.ShapeDtypeStruct((bt, NAT_W), BF16),
        compiler_params=pltpu.CompilerParams(
            dimension_semantics=("arbitrary", "arbitrary", "arbitrary"), vmem_limit_bytes=VMEM_LIMIT),
        name="nat2d",
    )(qkv, qkv, qkv, gn, bias)


def _nat_bias_table(rpb, rows):
    kh = min(NAT_KH, rows)
    heads, nro, nco = rpb.shape
    period = 2 * GRID_W
    u = jnp.concatenate([rpb[:, :, NAT_KW - 1:].astype(F32),
                         jnp.full((heads, nro, period - nco), NEG, F32),
                         rpb[:, :, :NAT_KW - 1].astype(F32)], axis=-1)
    flat = jnp.tile(u, (1, 1, GRID_W))[:, :, :GRID_W * (period - 1)]
    t = flat.reshape(heads, nro, GRID_W, period - 1)[:, :, :, :GRID_W]
    cols = np.arange(GRID_W)
    col_start = np.clip(cols - NAT_KW // 2, 0, GRID_W - NAT_KW)
    valid = (cols[None, :] >= col_start[:, None]) & (cols[None, :] < col_start[:, None] + NAT_KW)
    t = jnp.where(valid[None, None], t, NEG)
    t = jnp.transpose(t, (0, 2, 1, 3))
    t = jnp.stack([t[:, :, pi:pi + kh] for pi in range(NAT_KH)])
    return t.reshape(NAT_KH, heads // 2, 2 * GRID_W, kh * GRID_W)


def _out_kernel(yf_ref, yr_ref, bonus_ref, grw_ref, yb_ref, x_ref, p_ref, lnw_ref, lnb_ref,
                wout_ref, pg_ref, wgate_ref, wproj_ref, fg_ref, o_ref):
    r1 = lax.broadcasted_iota(jnp.int32, (LANES, LANES), 0)
    c1 = lax.broadcasted_iota(jnp.int32, (LANES, LANES), 1)
    avg_bd = jnp.where((r1 // HEAD) == (c1 // HEAD), 1.0 / HEAD, 0.0).astype(BF16)
    parts = []
    for p in range(PAIRS):
        y = yf_ref[p] + yr_ref[p]
        mu = _dot(y.astype(BF16), avg_bd)
        dlt = y - mu
        var = _dot((dlt * dlt).astype(BF16), avg_bd)
        yn = dlt * lax.rsqrt(var + LNX_EPS) * lnw_ref[p] + lnb_ref[p]
        g = grw_ref[p]
        parts.append(((yn + bonus_ref[p]) * (g * _sigmoid(g))).astype(BF16))
    parts.append(yb_ref[...])
    mix = jnp.concatenate(parts, axis=1)
    h = x_ref[...] + _dot(mix, wout_ref[...])

    def rms(v, gain):
        ms = jnp.mean(v * v, axis=-1, keepdims=True)
        return v * lax.rsqrt(ms + NORM_EPS) * gain

    gate = _sigmoid(_dot(rms(h, pg_ref[...]).astype(BF16), wgate_ref[...]))
    h = h + _dot(p_ref[...].astype(BF16), wproj_ref[...]) * gate
    o_ref[...] = rms(h, fg_ref[...])


def _out_stage(yf, yr, bonus, grw, yb, x2, p2, lnw, lnb, w_out, ple_g, w_gate, w_proj, final_g, *, tm):
    bt, d = x2.shape
    ple = p2.shape[1]
    const = lambda shape: pl.BlockSpec(shape, lambda i: (0,) * len(shape), pipeline_mode=pl.Buffered(1))
    pair_spec = pl.BlockSpec((PAIRS, tm, LANES), lambda i: (0, i, 0))
    return pl.pallas_call(
        _out_kernel,
        grid=(bt // tm,),
        in_specs=[
            pair_spec, pair_spec, pair_spec, pair_spec,
            pl.BlockSpec((tm, NAT_W), lambda i: (i, 0)),
            pl.BlockSpec((tm, d), lambda i: (i, 0)),
            pl.BlockSpec((tm, ple), lambda i: (i, 0)),
            const((PAIRS, 1, LANES)), const((PAIRS, 1, LANES)),
            const((RWKV_W + NAT_W, d)), const((1, d)), const((d, d)), const((ple, d)), const((1, d)),
        ],
        out_specs=pl.BlockSpec((tm, d), lambda i: (i, 0)),
        out_shape=jax.ShapeDtypeStruct((bt, d), F32),
        compiler_params=pltpu.CompilerParams(
            dimension_semantics=("arbitrary",), vmem_limit_bytes=VMEM_LIMIT),
        name="out_stage",
    )(yf, yr, bonus, grw, yb, x2, p2, lnw, lnb, w_out, ple_g, w_gate, w_proj, final_g)


INPROJ_TM = 512
OUT_TM = 256


def kernel(x, p, norm_mix_g, w_in, shift_mu_prev, shift_mu_next, decay_w0, decay_w2, iclr_a0, iclr_a2,
           k_k, k_a, r_k, lnx_w, lnx_b, nat_rpb, w_out, ple_norm_g, w_ple_gate, w_ple_proj, final_norm_g):
    batch, seq, d = x.shape
    depth = p.shape[0]
    bt = batch * seq
    rows = seq // GRID_W
    o_wd = 3 * RWKV_W
    shift_w = o_wd + 4 * LORA
    pairs = lambda a: a.reshape(PAIRS, 1, LANES).astype(F32)

    assert depth == 1, "the output stage fuses the final norm: one trunk layer only"
    h = x.reshape(bt, d)
    w = w_in[0]
    w_shift = w[:, :o_wd].astype(BF16)
    w_lora = w[:, o_wd:shift_w].astype(BF16)
    w_rest = w[:, shift_w:].astype(BF16)
    mup, mun = shift_mu_prev[0], shift_mu_next[0]
    zrkv, grw, qkv, gn, lora = _inproj(
        h, norm_mix_g[0].reshape(1, d), w_shift, w_rest, w_lora,
        mup[:o_wd].reshape(3, 1, RWKV_W), mun[:o_wd].reshape(3, 1, RWKV_W),
        mup[o_wd:].reshape(1, 4 * LORA), mun[o_wd:].reshape(1, 4 * LORA),
        seq=seq, tm=INPROJ_TM)
    yf, yr, bonus = _wkv(
        zrkv, lora,
        decay_w0[0].reshape(2, 1, RWKV_W), decay_w2[0].astype(BF16),
        iclr_a0[0].reshape(2, 1, RWKV_W), iclr_a2[0].astype(BF16),
        pairs(k_k[0]), pairs(k_a[0]), pairs(r_k[0]), batch=batch, seq=seq)
    yb = _nat(qkv, gn, _nat_bias_table(nat_rpb[0], rows), batch=batch, seq=seq)
    out = _out_stage(
        yf, yr, bonus, grw, yb, h, p[0].reshape(bt, -1), pairs(lnx_w[0]), pairs(lnx_b[0]),
        w_out[0].astype(BF16), ple_norm_g[0].reshape(1, d), w_ple_gate[0].astype(BF16),
        w_ple_proj[0].astype(BF16), final_norm_g.reshape(1, d), tm=OUT_TM)
    return out.reshape(batch, seq, d)
```

```python
import functools
import math

import numpy as np
import jax
import jax.numpy as jnp
from jax import lax
from jax.experimental import pallas as pl
from jax.experimental.pallas import tpu as pltpu

GRID_W = 64
HEAD = 64
RWKV_W = 1024
NAT_W = 1024
LORA = 64
NAT_KH = 8
NAT_KW = 16
NORM_EPS = 1e-6
LNX_EPS = 64e-5
DECAY_SCALE = math.exp(-0.5)

LANES = 128
PACK_ROWS = 16
VMEM_LIMIT = 56 * 1024 * 1024

CHUNK = 64
SUB = CHUNK // 2
SUB_LOG2 = SUB.bit_length() - 1
assert SUB == 1 << SUB_LOG2
PAIRS = RWKV_W // LANES
NEG = -1e30

F32 = jnp.float32
BF16 = jnp.bfloat16


def _dot(a, b):
    return jnp.dot(a, b, preferred_element_type=F32)


def _dot_nt(a, b):
    return lax.dot_general(a, b, (((1,), (1,)), ((), ())), preferred_element_type=F32)


def _dot_tn(a, b):
    return lax.dot_general(a, b, (((0,), (0,)), ((), ())), preferred_element_type=F32)


def _sigmoid(x):
    return 1.0 / (1.0 + jnp.exp(-x))


def _round_robin(chains):
    chains = list(chains)
    while chains:
        alive = []
        for c in chains:
            try:
                next(c)
                alive.append(c)
            except StopIteration:
                pass
        chains = alive


def _inproj_kernel(x_ref, xp_ref, xn_ref, g_ref, wa_ref, wb_ref, wl_ref, mup_ref, mun_ref, mupl_ref,
                   munl_ref, z4_ref, qk_ref, vn_ref, gn_ref, lora_ref, hn_ref, *, tm, tiles_per_batch):
    i = pl.program_id(0)
    j = pl.program_id(1)
    halo = PACK_ROWS
    rows = tm + 2 * halo

    def shifted(z, mup, mun):
        zp = pltpu.roll(z, 1, 0)[halo:halo + tm]
        zn = pltpu.roll(z, rows - 1, 0)[halo:halo + tm]
        zc = z[halo:halo + tm]
        return zc + mup * (zp - zc) + mun * (zn - zc)

    def store_pairs(slot, z):
        for p in range(PAIRS):
            z4_ref[slot, p] = z[:, p * LANES:(p + 1) * LANES]

    def centre(w_ref):
        return _dot(hn_ref[halo:halo + tm], w_ref[...])

    @pl.when(j == 0)
    def _():
        g = g_ref[...]

        def norm(xv):
            ms = jnp.mean(xv * xv, axis=-1, keepdims=True)
            return xv * lax.rsqrt(ms + NORM_EPS) * g

        ti = i % tiles_per_batch
        keep_p = (ti > 0).astype(F32)
        keep_n = (ti < tiles_per_batch - 1).astype(F32)
        hn_ref[0:halo] = (norm(xp_ref[...]) * keep_p).astype(BF16)
        hn_ref[halo:halo + tm] = norm(x_ref[...]).astype(BF16)
        hn_ref[halo + tm:rows] = (norm(xn_ref[...]) * keep_n).astype(BF16)
        zl = _dot(hn_ref[...], wl_ref[...])
        lora_ref[...] = shifted(zl, mupl_ref[...], munl_ref[...])
        store_pairs(0, shifted(_dot(hn_ref[...], wa_ref[...]), mup_ref[0], mun_ref[0]))
        store_pairs(1, shifted(_dot(hn_ref[...], wb_ref[...]), mup_ref[1], mun_ref[1]))

    @pl.when(j == 1)
    def _():
        store_pairs(0, shifted(_dot(hn_ref[...], wa_ref[...]), mup_ref[2], mun_ref[2]))
        store_pairs(1, centre(wb_ref))

    @pl.when(j == 2)
    def _():
        qk_ref[0] = centre(wa_ref).astype(BF16)
        qk_ref[1] = centre(wb_ref).astype(BF16)

    @pl.when(j == 3)
    def _():
        vn_ref[...] = centre(wa_ref).astype(BF16)
        gn_ref[...] = centre(wb_ref)


def _inproj(x2, norm_g, w_main, w_lora, mup, mun, mupl, munl, *, seq, tm):
    bt, d = x2.shape
    halo = PACK_ROWS
    tiles_per_batch = seq // tm
    n_i = bt // tm
    hb = tm // halo
    last_hb = bt // halo - 1
    n_j = w_main.shape[1] // (2 * RWKV_W)
    const = lambda shape: pl.BlockSpec(shape, lambda i, j: (0,) * len(shape))
    kernel = functools.partial(_inproj_kernel, tm=tm, tiles_per_batch=tiles_per_batch)
    return pl.pallas_call(
        kernel,
        grid=(n_i, n_j),
        in_specs=[
            pl.BlockSpec((tm, d), lambda i, j: (i, 0)),
            pl.BlockSpec((halo, d), lambda i, j: (jnp.maximum(i * hb - 1, 0), 0)),
            pl.BlockSpec((halo, d), lambda i, j: (jnp.minimum((i + 1) * hb, last_hb), 0)),
            const((1, d)),
            pl.BlockSpec((d, RWKV_W), lambda i, j: (0, 2 * j)),
            pl.BlockSpec((d, RWKV_W), lambda i, j: (0, 2 * j + 1)),
            const((d, 4 * LORA)),
            const((3, 1, RWKV_W)), const((3, 1, RWKV_W)),
            const((1, 4 * LORA)), const((1, 4 * LORA)),
        ],
        out_specs=[
            pl.BlockSpec((2, PAIRS, tm, LANES), lambda i, j: (jnp.minimum(j, 1), 0, i, 0)),
            pl.BlockSpec((2, tm, NAT_W), lambda i, j: (0, i, 0)),
            pl.BlockSpec((tm, NAT_W), lambda i, j: (i, 0)),
            pl.BlockSpec((tm, NAT_W), lambda i, j: (i, 0)),
            pl.BlockSpec((tm, 4 * LORA), lambda i, j: (i, 0)),
        ],
        out_shape=[
            jax.ShapeDtypeStruct((4, PAIRS, bt, LANES), F32),
            jax.ShapeDtypeStruct((2, bt, NAT_W), BF16),
            jax.ShapeDtypeStruct((bt, NAT_W), BF16),
            jax.ShapeDtypeStruct((bt, NAT_W), F32),
            jax.ShapeDtypeStruct((bt, 4 * LORA), F32),
        ],
        scratch_shapes=[pltpu.VMEM((tm + 2 * halo, d), BF16)],
        compiler_params=pltpu.CompilerParams(
            dimension_semantics=("arbitrary", "arbitrary"), vmem_limit_bytes=VMEM_LIMIT),
        name="inproj",
    )(x2, x2, x2, norm_g, w_main, w_main, w_lora, mup, mun, mupl, munl)


def _wkv_kernel(zf_ref, zr_ref, lf_ref, lr_ref, w0_ref, w2_ref, a0_ref, a2_ref, kk_ref, ka_ref, rk_ref,
                yf_ref, yr_ref, bonus_ref, st_ref, cinc_ref, lw_ref, iclr_ref):
    L = CHUNK
    W4 = 2 * LANES
    nb = lf_ref.shape[0]

    @pl.when(pl.program_id(0) == 0)
    def _():
        st_ref[...] = jnp.zeros_like(st_ref)

    row = lax.broadcasted_iota(jnp.int32, (nb * L, nb * L), 0)
    col = lax.broadcasted_iota(jnp.int32, (nb * L, nb * L), 1)
    same = (row // L) == (col // L)
    for d, l_ref in ((0, lf_ref), (1, lr_ref)):
        lo = l_ref[...].reshape(nb * L, 4 * LORA)
        wd = lo[:, d * LORA:(d + 1) * LORA]
        ad = lo[:, (2 + d) * LORA:(3 + d) * LORA]
        dw = _dot(jnp.tanh(wd).astype(BF16), w2_ref[d])
        lw = -DECAY_SCALE * _sigmoid(w0_ref[d] + dw)
        da = _dot(ad.astype(BF16), a2_ref[d])
        ic = _sigmoid(a0_ref[d] + da)
        tri = (same & ((row >= col) if d == 0 else (col >= row))).astype(BF16)
        hi = lw.astype(BF16)
        lo2 = (lw - hi.astype(F32)).astype(BF16)
        cinc = _dot(tri, hi) + _dot(tri, lo2)
        for b in range(nb):
            for p in range(PAIRS):
                sl = slice(p * LANES, (p + 1) * LANES)
                cinc_ref[d, b, p] = cinc[b * L:(b + 1) * L, sl]
                lw_ref[d, b, p] = lw[b * L:(b + 1) * L, sl]
                iclr_ref[d, b, p] = ic[b * L:(b + 1) * L, sl]

    ri = lax.broadcasted_iota(jnp.int32, (L, W4), 0)
    li = lax.broadcasted_iota(jnp.int32, (L, W4), 1)
    blk = li // HEAD
    jj = li % HEAD
    fwd = li < LANES
    strict = (fwd & (jj < ri)) | (~fwd & (jj > ri))
    incl = (fwd & (jj <= ri)) | (~fwd & (jj >= ri))
    eye = (jj == ri).astype(F32)
    diag_blk = (jj // SUB) == (ri // SUB)
    r1 = lax.broadcasted_iota(jnp.int32, (LANES, LANES), 0)
    c1 = lax.broadcasted_iota(jnp.int32, (LANES, LANES), 1)
    ones_bd_mask = (r1 // HEAD) == (c1 // HEAD)
    ones_bd = ones_bd_mask.astype(BF16)

    def bd(x):
        xb = x.astype(BF16)
        z = jnp.zeros_like(xb)
        return jnp.concatenate([jnp.where(blk == q, xb, z) for q in range(4)], axis=0)

    def mm(a, b):
        return _dot(a.astype(BF16), bd(b))

    def pair_chain(b, p):
        kkp = kk_ref[p]
        kap = ka_ref[p]
        rkp = rk_ref[p]
        sums = []
        for d, z_ref in ((0, zf_ref), (1, zr_ref)):
            kraw = z_ref[1, p, b] * kkp
            sums.append(_dot((kraw * kraw).astype(BF16), ones_bd))
        rk = _dot((zf_ref[0, p, b] * zf_ref[1, p, b] * rkp).astype(BF16), ones_bd)
        yield
        bonus_ref[p, b] = rk * zf_ref[2, p, b]
        rt, at, bt, kt, bh, kh, vv, pl_tot = [], [], [], [], [], [], [], []
        for d, z_ref in ((0, zf_ref), (1, zr_ref)):
            r = z_ref[0, p, b]
            k = z_ref[1, p, b]
            ci = cinc_ref[d, b, p]
            lw = lw_ref[d, b, p]
            ic = iclr_ref[d, b, p]
            kk = (k * kkp) * lax.rsqrt(jnp.maximum(sums[d], 1e-24))
            kd = k * (1.0 + (ic - 1.0) * kap)
            kb = kk * ic
            ctot = ci[L - 1:L] if d == 0 else ci[0:1]
            pinv = jnp.exp(-ci)
            ptot = jnp.exp(ctot)
            bti = kb * pinv
            kti = kd * pinv
            rt.append((r * jnp.exp(ci)).astype(BF16))
            at.append((-kk * jnp.exp(ci - lw)).astype(BF16))
            bt.append(bti.astype(BF16))
            kt.append(kti.astype(BF16))
            bh.append((bti * ptot).astype(BF16))
            kh.append((kti * ptot).astype(BF16))
            vv.append(z_ref[2, p, b].astype(BF16))
            pl_tot.append(ptot)

        cat = lambda xs: jnp.concatenate(xs, axis=1)
        rt, at, bt, kt, bh, kh, vv, pl_tot = map(cat, (rt, at, bt, kt, bh, kh, vv, pl_tot))
        bhkh = jnp.concatenate([bh, kh], axis=0)

        o1 = _dot(jnp.concatenate([at, rt], axis=0),
                  jnp.concatenate([bd(bt).T, bd(kt).T], axis=1))
        yield
        a_ab = jnp.where(strict, o1[0:L, 0:W4], 0.0)
        a_ak = jnp.where(strict, o1[0:L, W4:2 * W4], 0.0)
        a_r = jnp.concatenate([jnp.where(incl, o1[L:2 * L, 0:W4], 0.0),
                               jnp.where(incl, o1[L:2 * L, W4:2 * W4], 0.0)], axis=1).astype(BF16)

        dg = jnp.where(diag_blk, a_ab, 0.0)
        off = (a_ab - dg).astype(BF16)
        tn = eye + dg
        pw = mm(dg, dg)
        av = mm(a_ak, vv)
        yield
        for _ in range(SUB_LOG2 - 2):
            o = mm(jnp.concatenate([tn, pw], axis=0), pw)
            yield
            tn = tn + o[0:L]
            pw = o[L:2 * L]
        o = mm(tn, pw)
        yield
        td = tn + o
        f = mm(td, off)
        ar = jnp.concatenate([at, rt], axis=0)
        o5 = jnp.concatenate(
            [_dot_nt(ar[:, d * LANES:(d + 1) * LANES], st_ref[b, p, d].astype(BF16))
             for d in range(2)], axis=1)
        yield
        o = mm(f, td)
        yield
        tinv = td + o
        o = mm(tinv, o5[0:L] + av)
        yield
        u = o.astype(BF16)
        o6 = _dot(a_r, jnp.concatenate([bd(u), bd(vv)], axis=0))
        uvv = jnp.concatenate([u, vv], axis=0)
        gs = [_dot_tn(uvv[:, d * LANES:(d + 1) * LANES], bhkh[:, d * LANES:(d + 1) * LANES])
              for d in range(2)]
        yield
        y = o5[L:2 * L] + o6
        for d in range(2):
            st_ref[b, p, d] = (st_ref[b, p, d] * pl_tot[:, d * LANES:(d + 1) * LANES]
                               + jnp.where(ones_bd_mask, gs[d], 0.0))
        yf_ref[p, b] = y[:, 0:LANES]
        yr_ref[p, b] = y[:, LANES:2 * LANES]

    _round_robin([pair_chain(b, p) for b in range(nb) for p in range(PAIRS)])


def _wkv(z4, lora, w0, w2, a0, a2, k_k, k_a, r_k, *, batch, seq):
    nc = seq // CHUNK
    z5 = z4.reshape(4, PAIRS, batch, seq, LANES)
    lora3 = lora.reshape(batch, seq, 4 * LORA)
    full = lambda shape: pl.BlockSpec(shape, lambda i: (0,) * len(shape))
    z_blk = (3, PAIRS, batch, CHUNK, LANES)
    y_blk = (PAIRS, batch, CHUNK, LANES)
    scr = (2, batch, PAIRS, CHUNK, LANES)
    yf, yr, bonus = pl.pallas_call(
        _wkv_kernel,
        grid=(nc,),
        in_specs=[
            pl.BlockSpec(z_blk, lambda i: (0, 0, 0, i, 0)),
            pl.BlockSpec(z_blk, lambda i: (0, 0, 0, nc - 1 - i, 0)),
            pl.BlockSpec((batch, CHUNK, 4 * LORA), lambda i: (0, i, 0)),
            pl.BlockSpec((batch, CHUNK, 4 * LORA), lambda i: (0, nc - 1 - i, 0)),
            full((2, 1, RWKV_W)), full((2, LORA, RWKV_W)),
            full((2, 1, RWKV_W)), full((2, LORA, RWKV_W)),
            full((PAIRS, 1, LANES)), full((PAIRS, 1, LANES)), full((PAIRS, 1, LANES)),
        ],
        out_specs=[
            pl.BlockSpec(y_blk, lambda i: (0, 0, i, 0)),
            pl.BlockSpec(y_blk, lambda i: (0, 0, nc - 1 - i, 0)),
            pl.BlockSpec(y_blk, lambda i: (0, 0, i, 0)),
        ],
        out_shape=[jax.ShapeDtypeStruct((PAIRS, batch, seq, LANES), F32)] * 3,
        scratch_shapes=[
            pltpu.VMEM((batch, PAIRS, 2, LANES, LANES), F32),
            pltpu.VMEM(scr, F32),
            pltpu.VMEM(scr, F32),
            pltpu.VMEM(scr, F32),
        ],
        compiler_params=pltpu.CompilerParams(
            dimension_semantics=("arbitrary",), vmem_limit_bytes=VMEM_LIMIT),
        name="wkv7",
    )(z5, z5, lora3, lora3, w0, w2, a0, a2, k_k, k_a, r_k)
    flat = lambda a: a.reshape(PAIRS, batch * seq, LANES)
    return flat(yf), flat(yr), flat(bonus)


NAT_ROWS_PER_STEP = 16
NAT_GROUP_W = 2 * LANES


def _nat_kernel(q_ref, k_ref, v_ref, g_ref, bias_ref, o_ref, *, rows):
    rb = pl.program_id(2)
    kh = min(NAT_KH, rows)
    win = kh * GRID_W
    scale = HEAD ** -0.5
    assert math.frexp(scale)[0] == 0.5, "scale is folded into bf16 q: must be a power of two"
    lane = lax.broadcasted_iota(jnp.int32, (GRID_W, LANES), 1)
    low = lane < HEAD

    def chain(qr, pp):
        r = rb * NAT_ROWS_PER_STEP + qr
        rs = jnp.clip(r - kh // 2, 0, rows - kh)
        pidx = rs - r + (NAT_KH - 1)
        q0 = qr * GRID_W
        k0 = pl.multiple_of(rs * GRID_W, GRID_W)
        sl = slice(pp * LANES, (pp + 1) * LANES)
        qp = q_ref[0, q0:q0 + GRID_W, sl]
        zq = jnp.zeros_like(qp)
        qs = jnp.concatenate([jnp.where(low, qp, zq), jnp.where(low, zq, qp)], axis=0)
        s = _dot_nt(qs, k_ref[0, pl.ds(k0, win), sl])
        yield
        s = s * scale + bias_ref[pidx, pp]
        m = jnp.max(s, axis=-1, keepdims=True)
        e = jnp.exp(s - m)
        l = jnp.sum(e, axis=-1, keepdims=True)
        o2 = _dot(e.astype(BF16), v_ref[pl.ds(k0, win), sl])
        yield
        o2 = o2 / l
        o = jnp.where(low, o2[0:GRID_W], o2[GRID_W:2 * GRID_W])
        g = g_ref[q0:q0 + GRID_W, sl]
        o_ref[q0:q0 + GRID_W, sl] = (o * (g * _sigmoid(g))).astype(o_ref.dtype)

    _round_robin([chain(qr, pp) for qr in range(NAT_ROWS_PER_STEP)
                  for pp in range(NAT_GROUP_W // LANES)])


def _nat(qk, vn, gn, bias, *, batch, seq):
    bt = gn.shape[0]
    rows = seq // GRID_W
    kh = min(NAT_KH, rows)
    tq = NAT_ROWS_PER_STEP * GRID_W
    nrb = rows // NAT_ROWS_PER_STEP
    ngrp = NAT_W // NAT_GROUP_W
    ppg = NAT_GROUP_W // LANES
    kernel = functools.partial(_nat_kernel, rows=rows)
    return pl.pallas_call(
        kernel,
        grid=(batch, ngrp, nrb),
        in_specs=[
            pl.BlockSpec((1, tq, NAT_GROUP_W), lambda b, g, r: (0, b * nrb + r, g)),
            pl.BlockSpec((1, seq, NAT_GROUP_W), lambda b, g, r: (1, b, g)),
            pl.BlockSpec((seq, NAT_GROUP_W), lambda b, g, r: (b, g)),
            pl.BlockSpec((tq, NAT_GROUP_W), lambda b, g, r: (b * nrb + r, g)),
            pl.BlockSpec((NAT_KH, ppg, 2 * GRID_W, kh * GRID_W), lambda b, g, r: (0, g, 0, 0)),
        ],
        out_specs=pl.BlockSpec((tq, NAT_GROUP_W), lambda b, g, r: (b * nrb + r, g)),
        out_shape=jax.ShapeDtypeStruct((bt, NAT_W), BF16),
        compiler_params=pltpu.CompilerParams(
            dimension_semantics=("arbitrary", "arbitrary", "arbitrary"), vmem_limit_bytes=VMEM_LIMIT),
        name="nat2d",
    )(qk, qk, vn, gn, bias)


def _nat_bias_table(rpb, rows):
    kh = min(NAT_KH, rows)
    heads, nro, nco = rpb.shape
    period = 2 * GRID_W
    u = jnp.concatenate([rpb[:, :, NAT_KW - 1:].astype(F32),
                         jnp.full((heads, nro, period - nco), NEG, F32),
                         rpb[:, :, :NAT_KW - 1].astype(F32)], axis=-1)
    flat = jnp.tile(u, (1, 1, GRID_W))[:, :, :GRID_W * (period - 1)]
    t = flat.reshape(heads, nro, GRID_W, period - 1)[:, :, :, :GRID_W]
    cols = np.arange(GRID_W)
    col_start = np.clip(cols - NAT_KW // 2, 0, GRID_W - NAT_KW)
    valid = (cols[None, :] >= col_start[:, None]) & (cols[None, :] < col_start[:, None] + NAT_KW)
    t = jnp.where(valid[None, None], t, NEG)
    t = jnp.transpose(t, (0, 2, 1, 3))
    t = jnp.stack([t[:, :, pi:pi + kh] for pi in range(NAT_KH)])
    return t.reshape(NAT_KH, heads // 2, 2 * GRID_W, kh * GRID_W)


def _out_kernel(yf_ref, yr_ref, bonus_ref, grw_ref, yb_ref, x_ref, p_ref, lnw_ref, lnb_ref,
                wout_ref, pg_ref, wgate_ref, wproj_ref, fg_ref, o_ref):
    r1 = lax.broadcasted_iota(jnp.int32, (LANES, LANES), 0)
    c1 = lax.broadcasted_iota(jnp.int32, (LANES, LANES), 1)
    avg_bd = jnp.where((r1 // HEAD) == (c1 // HEAD), 1.0 / HEAD, 0.0).astype(BF16)
    parts = []
    for p in range(PAIRS):
        y = yf_ref[p] + yr_ref[p]
        mu = _dot(y.astype(BF16), avg_bd)
        dlt = y - mu
        var = _dot((dlt * dlt).astype(BF16), avg_bd)
        yn = dlt * lax.rsqrt(var + LNX_EPS) * lnw_ref[p] + lnb_ref[p]
        g = grw_ref[0, p]
        parts.append(((yn + bonus_ref[p]) * (g * _sigmoid(g))).astype(BF16))
    parts.append(yb_ref[...])
    mix = jnp.concatenate(parts, axis=1)
    h = x_ref[...] + _dot(mix, wout_ref[...])

    def rms(v, gain):
        ms = jnp.mean(v * v, axis=-1, keepdims=True)
        return v * lax.rsqrt(ms + NORM_EPS) * gain

    gate = _sigmoid(_dot(rms(h, pg_ref[...]).astype(BF16), wgate_ref[...]))
    h = h + _dot(p_ref[...].astype(BF16), wproj_ref[...]) * gate
    o_ref[...] = rms(h, fg_ref[...])


def _out_stage(yf, yr, bonus, z4, yb, x2, p2, lnw, lnb, w_out, ple_g, w_gate, w_proj, final_g, *, tm):
    bt, d = x2.shape
    ple = p2.shape[1]
    const = lambda shape: pl.BlockSpec(shape, lambda i: (0,) * len(shape), pipeline_mode=pl.Buffered(1))
    pair_spec = pl.BlockSpec((PAIRS, tm, LANES), lambda i: (0, i, 0))
    return pl.pallas_call(
        _out_kernel,
        grid=(bt // tm,),
        in_specs=[
            pair_spec, pair_spec, pair_spec,
            pl.BlockSpec((1, PAIRS, tm, LANES), lambda i: (3, 0, i, 0)),
            pl.BlockSpec((tm, NAT_W), lambda i: (i, 0)),
            pl.BlockSpec((tm, d), lambda i: (i, 0)),
            pl.BlockSpec((tm, ple), lambda i: (i, 0)),
            const((PAIRS, 1, LANES)), const((PAIRS, 1, LANES)),
            const((RWKV_W + NAT_W, d)), const((1, d)), const((d, d)), const((ple, d)), const((1, d)),
        ],
        out_specs=pl.BlockSpec((tm, d), lambda i: (i, 0)),
        out_shape=jax.ShapeDtypeStruct((bt, d), F32),
        compiler_params=pltpu.CompilerParams(
            dimension_semantics=("arbitrary",), vmem_limit_bytes=VMEM_LIMIT),
        name="out_stage",
    )(yf, yr, bonus, z4, yb, x2, p2, lnw, lnb, w_out, ple_g, w_gate, w_proj, final_g)


INPROJ_TM = 512
OUT_TM = 256


def kernel(x, p, norm_mix_g, w_in, shift_mu_prev, shift_mu_next, decay_w0, decay_w2, iclr_a0, iclr_a2,
           k_k, k_a, r_k, lnx_w, lnx_b, nat_rpb, w_out, ple_norm_g, w_ple_gate, w_ple_proj, final_norm_g):
    batch, seq, d = x.shape
    depth = p.shape[0]
    bt = batch * seq
    rows = seq // GRID_W
    o_wd = 3 * RWKV_W
    shift_w = o_wd + 4 * LORA
    pairs = lambda a: a.reshape(PAIRS, 1, LANES).astype(F32)

    assert depth == 1, "the output stage fuses the final norm: one trunk layer only"
    h = x.reshape(bt, d)
    w = w_in[0]
    w_main = jnp.concatenate([w[:, :o_wd], w[:, shift_w:]], axis=1).astype(BF16)
    w_lora = w[:, o_wd:shift_w].astype(BF16)
    mup, mun = shift_mu_prev[0], shift_mu_next[0]
    z4, qk, vn, gn, lora = _inproj(
        h, norm_mix_g[0].reshape(1, d), w_main, w_lora,
        mup[:o_wd].reshape(3, 1, RWKV_W), mun[:o_wd].reshape(3, 1, RWKV_W),
        mup[o_wd:].reshape(1, 4 * LORA), mun[o_wd:].reshape(1, 4 * LORA),
        seq=seq, tm=INPROJ_TM)
    yf, yr, bonus = _wkv(
        z4, lora,
        decay_w0[0].reshape(2, 1, RWKV_W), decay_w2[0].astype(BF16),
        iclr_a0[0].reshape(2, 1, RWKV_W), iclr_a2[0].astype(BF16),
        pairs(k_k[0]), pairs(k_a[0]), pairs(r_k[0]), batch=batch, seq=seq)
    yb = _nat(qk, vn, gn, _nat_bias_table(nat_rpb[0], rows), batch=batch, seq=seq)
    out = _out_stage(
        yf, yr, bonus, z4, yb, h, p[0].reshape(bt, -1), pairs(lnx_w[0]), pairs(lnx_b[0]),
        w_out[0].astype(BF16), ple_norm_g[0].reshape(1, d), w_ple_gate[0].astype(BF16),
        w_ple_proj[0].astype(BF16), final_norm_g.reshape(1, d), tm=OUT_TM)
    return out.reshape(batch, seq, d)
```

```python
import functools
import math

import numpy as np
import jax
import jax.numpy as jnp
from jax import lax
from jax.experimental import pallas as pl
from jax.experimental.pallas import tpu as pltpu

GRID_W = 64
HEAD = 64
RWKV_W = 1024
NAT_W = 1024
LORA = 64
NAT_KH = 8
NAT_KW = 16
NORM_EPS = 1e-6
LNX_EPS = 64e-5
DECAY_SCALE = math.exp(-0.5)

LANES = 128
PACK_ROWS = 16
VMEM_LIMIT = 56 * 1024 * 1024

CHUNK = 64
SUB = CHUNK // 2
SUB_LOG2 = SUB.bit_length() - 1
assert SUB == 1 << SUB_LOG2
PAIRS = RWKV_W // LANES
WKV_CHUNKS_PER_STEP = 2
NEG = -1e30

F32 = jnp.float32
BF16 = jnp.bfloat16


def _dot(a, b):
    return jnp.dot(a, b, preferred_element_type=F32)


def _dot_nt(a, b):
    return lax.dot_general(a, b, (((1,), (1,)), ((), ())), preferred_element_type=F32)


def _dot_tn(a, b):
    return lax.dot_general(a, b, (((0,), (0,)), ((), ())), preferred_element_type=F32)


def _sigmoid(x):
    return 1.0 / (1.0 + jnp.exp(-x))


def _round_robin(chains):
    chains = list(chains)
    while chains:
        alive = []
        for c in chains:
            try:
                next(c)
                alive.append(c)
            except StopIteration:
                pass
        chains = alive


def _inproj_kernel(x_ref, xp_ref, xn_ref, g_ref, wa_ref, wb_ref, wl_ref, mup_ref, mun_ref, mupl_ref,
                   munl_ref, z4_ref, qk_ref, vn_ref, gn_ref, lora_ref, hn_ref, *, tm, tiles_per_batch):
    i = pl.program_id(0)
    j = pl.program_id(1)
    halo = PACK_ROWS
    rows = tm + 2 * halo

    def shifted(z, mup, mun):
        zp = pltpu.roll(z, 1, 0)[halo:halo + tm]
        zn = pltpu.roll(z, rows - 1, 0)[halo:halo + tm]
        zc = z[halo:halo + tm]
        return zc + mup * (zp - zc) + mun * (zn - zc)

    def store_pairs(slot, z):
        for p in range(PAIRS):
            z4_ref[slot, p] = z[:, p * LANES:(p + 1) * LANES]

    def centre(w_ref):
        return _dot(hn_ref[halo:halo + tm], w_ref[...])

    @pl.when(j == 0)
    def _():
        g = g_ref[...]

        def norm(xv):
            ms = jnp.mean(xv * xv, axis=-1, keepdims=True)
            return xv * lax.rsqrt(ms + NORM_EPS) * g

        ti = i % tiles_per_batch
        keep_p = (ti > 0).astype(F32)
        keep_n = (ti < tiles_per_batch - 1).astype(F32)
        hn_ref[0:halo] = (norm(xp_ref[...]) * keep_p).astype(BF16)
        hn_ref[halo:halo + tm] = norm(x_ref[...]).astype(BF16)
        hn_ref[halo + tm:rows] = (norm(xn_ref[...]) * keep_n).astype(BF16)
        zl = _dot(hn_ref[...], wl_ref[...])
        lora_ref[...] = shifted(zl, mupl_ref[...], munl_ref[...])
        store_pairs(0, shifted(_dot(hn_ref[...], wa_ref[...]), mup_ref[0], mun_ref[0]))
        store_pairs(1, shifted(_dot(hn_ref[...], wb_ref[...]), mup_ref[1], mun_ref[1]))

    @pl.when(j == 1)
    def _():
        store_pairs(0, shifted(_dot(hn_ref[...], wa_ref[...]), mup_ref[2], mun_ref[2]))
        store_pairs(1, centre(wb_ref))

    @pl.when(j == 2)
    def _():
        qk_ref[0] = centre(wa_ref).astype(BF16)
        qk_ref[1] = centre(wb_ref).astype(BF16)

    @pl.when(j == 3)
    def _():
        vn_ref[...] = centre(wa_ref).astype(BF16)
        gn_ref[...] = centre(wb_ref)


def _inproj(x2, norm_g, w_main, w_lora, mup, mun, mupl, munl, *, seq, tm):
    bt, d = x2.shape
    halo = PACK_ROWS
    tiles_per_batch = seq // tm
    n_i = bt // tm
    hb = tm // halo
    last_hb = bt // halo - 1
    n_j = w_main.shape[1] // (2 * RWKV_W)
    const = lambda shape: pl.BlockSpec(shape, lambda i, j: (0,) * len(shape))
    kernel = functools.partial(_inproj_kernel, tm=tm, tiles_per_batch=tiles_per_batch)
    return pl.pallas_call(
        kernel,
        grid=(n_i, n_j),
        in_specs=[
            pl.BlockSpec((tm, d), lambda i, j: (i, 0)),
            pl.BlockSpec((halo, d), lambda i, j: (jnp.maximum(i * hb - 1, 0), 0)),
            pl.BlockSpec((halo, d), lambda i, j: (jnp.minimum((i + 1) * hb, last_hb), 0)),
            const((1, d)),
            pl.BlockSpec((d, RWKV_W), lambda i, j: (0, 2 * j)),
            pl.BlockSpec((d, RWKV_W), lambda i, j: (0, 2 * j + 1)),
            const((d, 4 * LORA)),
            const((3, 1, RWKV_W)), const((3, 1, RWKV_W)),
            const((1, 4 * LORA)), const((1, 4 * LORA)),
        ],
        out_specs=[
            pl.BlockSpec((2, PAIRS, tm, LANES), lambda i, j: (jnp.minimum(j, 1), 0, i, 0)),
            pl.BlockSpec((2, tm, NAT_W), lambda i, j: (0, i, 0)),
            pl.BlockSpec((tm, NAT_W), lambda i, j: (i, 0)),
            pl.BlockSpec((tm, NAT_W), lambda i, j: (i, 0)),
            pl.BlockSpec((tm, 4 * LORA), lambda i, j: (i, 0)),
        ],
        out_shape=[
            jax.ShapeDtypeStruct((4, PAIRS, bt, LANES), F32),
            jax.ShapeDtypeStruct((2, bt, NAT_W), BF16),
            jax.ShapeDtypeStruct((bt, NAT_W), BF16),
            jax.ShapeDtypeStruct((bt, NAT_W), F32),
            jax.ShapeDtypeStruct((bt, 4 * LORA), F32),
        ],
        scratch_shapes=[pltpu.VMEM((tm + 2 * halo, d), BF16)],
        compiler_params=pltpu.CompilerParams(
            dimension_semantics=("arbitrary", "arbitrary"), vmem_limit_bytes=VMEM_LIMIT),
        name="inproj",
    )(x2, x2, x2, norm_g, w_main, w_main, w_lora, mup, mun, mupl, munl)


def _wkv_kernel(zf_ref, zr_ref, lf_ref, lr_ref, w0_ref, w2_ref, a0_ref, a2_ref, kk_ref, ka_ref, rk_ref,
                yf_ref, yr_ref, bonus_ref, st_ref, cinc_ref, lw_ref, iclr_ref):
    L = CHUNK
    W4 = 2 * LANES
    nb = lf_ref.shape[0]
    nsub = lf_ref.shape[1] // L
    nblk = nb * nsub

    @pl.when(pl.program_id(0) == 0)
    def _():
        st_ref[...] = jnp.zeros_like(st_ref)

    row = lax.broadcasted_iota(jnp.int32, (nblk * L, nblk * L), 0)
    col = lax.broadcasted_iota(jnp.int32, (nblk * L, nblk * L), 1)
    same = (row // L) == (col // L)
    for d, l_ref in ((0, lf_ref), (1, lr_ref)):
        lo = l_ref[...].reshape(nblk * L, 4 * LORA)
        wd = lo[:, d * LORA:(d + 1) * LORA]
        ad = lo[:, (2 + d) * LORA:(3 + d) * LORA]
        dw = _dot(jnp.tanh(wd).astype(BF16), w2_ref[d])
        lw = -DECAY_SCALE * _sigmoid(w0_ref[d] + dw)
        da = _dot(ad.astype(BF16), a2_ref[d])
        ic = _sigmoid(a0_ref[d] + da)
        tri = (same & ((row >= col) if d == 0 else (col >= row))).astype(BF16)
        hi = lw.astype(BF16)
        lo2 = (lw - hi.astype(F32)).astype(BF16)
        cinc = _dot(tri, hi) + _dot(tri, lo2)
        for q in range(nblk):
            for p in range(PAIRS):
                sl = slice(p * LANES, (p + 1) * LANES)
                cinc_ref[d, q, p] = cinc[q * L:(q + 1) * L, sl]
                lw_ref[d, q, p] = lw[q * L:(q + 1) * L, sl]
                iclr_ref[d, q, p] = ic[q * L:(q + 1) * L, sl]

    ri = lax.broadcasted_iota(jnp.int32, (L, W4), 0)
    li = lax.broadcasted_iota(jnp.int32, (L, W4), 1)
    blk = li // HEAD
    jj = li % HEAD
    fwd = li < LANES
    strict = (fwd & (jj < ri)) | (~fwd & (jj > ri))
    incl = (fwd & (jj <= ri)) | (~fwd & (jj >= ri))
    eye = (jj == ri).astype(F32)
    diag_blk = (jj // SUB) == (ri // SUB)
    r1 = lax.broadcasted_iota(jnp.int32, (LANES, LANES), 0)
    c1 = lax.broadcasted_iota(jnp.int32, (LANES, LANES), 1)
    ones_bd_mask = (r1 // HEAD) == (c1 // HEAD)
    ones_bd = ones_bd_mask.astype(BF16)

    def bd(x):
        xb = x.astype(BF16)
        z = jnp.zeros_like(xb)
        return jnp.concatenate([jnp.where(blk == q, xb, z) for q in range(4)], axis=0)

    def mm(a, b):
        return _dot(a.astype(BF16), bd(b))

    state_written = set()

    def pair_chain(b, p, c):
        half = (c, nsub - 1 - c)
        rows = [slice(h * L, (h + 1) * L) for h in half]
        zf = lambda i: zf_ref[i, p, b, rows[0]]
        zr = lambda i: zr_ref[i, p, b, rows[1]]
        kkp = kk_ref[p]
        kap = ka_ref[p]
        rkp = rk_ref[p]
        sums = []
        for z in (zf, zr):
            kraw = z(1) * kkp
            sums.append(_dot((kraw * kraw).astype(BF16), ones_bd))
        rk = _dot((zf(0) * zf(1) * rkp).astype(BF16), ones_bd)
        yield
        bonus_ref[p, b, rows[0]] = rk * zf(2)
        rt, at, bt, kt, bh, kh, vv, pl_tot = [], [], [], [], [], [], [], []
        for d, z in ((0, zf), (1, zr)):
            r = z(0)
            k = z(1)
            q = b * nsub + half[d]
            ci = cinc_ref[d, q, p]
            lw = lw_ref[d, q, p]
            ic = iclr_ref[d, q, p]
            kk = (k * kkp) * lax.rsqrt(jnp.maximum(sums[d], 1e-24))
            kd = k * (1.0 + (ic - 1.0) * kap)
            kb = kk * ic
            ctot = ci[L - 1:L] if d == 0 else ci[0:1]
            pinv = jnp.exp(-ci)
            ptot = jnp.exp(ctot)
            bti = kb * pinv
            kti = kd * pinv
            rt.append((r * jnp.exp(ci)).astype(BF16))
            at.append((-kk * jnp.exp(ci - lw)).astype(BF16))
            bt.append(bti.astype(BF16))
            kt.append(kti.astype(BF16))
            bh.append((bti * ptot).astype(BF16))
            kh.append((kti * ptot).astype(BF16))
            vv.append(z(2).astype(BF16))
            pl_tot.append(ptot)

        cat = lambda xs: jnp.concatenate(xs, axis=1)
        rt, at, bt, kt, bh, kh, vv, pl_tot = map(cat, (rt, at, bt, kt, bh, kh, vv, pl_tot))
        bhkh = jnp.concatenate([bh, kh], axis=0)

        o1 = _dot(jnp.concatenate([at, rt], axis=0),
                  jnp.concatenate([bd(bt).T, bd(kt).T], axis=1))
        yield
        a_ab = jnp.where(strict, o1[0:L, 0:W4], 0.0)
        a_ak = jnp.where(strict, o1[0:L, W4:2 * W4], 0.0)
        a_r = jnp.concatenate([jnp.where(incl, o1[L:2 * L, 0:W4], 0.0),
                               jnp.where(incl, o1[L:2 * L, W4:2 * W4], 0.0)], axis=1).astype(BF16)

        dg = jnp.where(diag_blk, a_ab, 0.0)
        off = (a_ab - dg).astype(BF16)
        tn = eye + dg
        pw = mm(dg, dg)
        av = mm(a_ak, vv)
        yield
        for _ in range(SUB_LOG2 - 2):
            o = mm(jnp.concatenate([tn, pw], axis=0), pw)
            yield
            tn = tn + o[0:L]
            pw = o[L:2 * L]
        o = mm(tn, pw)
        yield
        td = tn + o
        f = mm(td, off)
        while c > 0 and (b, p, c - 1) not in state_written:
            yield
        ar = jnp.concatenate([at, rt], axis=0)
        o5 = jnp.concatenate(
            [_dot_nt(ar[:, d * LANES:(d + 1) * LANES], st_ref[b, p, d].astype(BF16))
             for d in range(2)], axis=1)
        yield
        o = mm(f, td)
        yield
        tinv = td + o
        o = mm(tinv, o5[0:L] + av)
        yield
        u = o.astype(BF16)
        o6 = _dot(a_r, jnp.concatenate([bd(u), bd(vv)], axis=0))
        uvv = jnp.concatenate([u, vv], axis=0)
        gs = [_dot_tn(uvv[:, d * LANES:(d + 1) * LANES], bhkh[:, d * LANES:(d + 1) * LANES])
              for d in range(2)]
        yield
        y = o5[L:2 * L] + o6
        for d in range(2):
            st_ref[b, p, d] = (st_ref[b, p, d] * pl_tot[:, d * LANES:(d + 1) * LANES]
                               + jnp.where(ones_bd_mask, gs[d], 0.0))
        state_written.add((b, p, c))
        yf_ref[p, b, rows[0]] = y[:, 0:LANES]
        yr_ref[p, b, rows[1]] = y[:, LANES:2 * LANES]

    _round_robin([pair_chain(b, p, c) for c in range(nsub) for b in range(nb) for p in range(PAIRS)])


def _wkv(z4, lora, w0, w2, a0, a2, k_k, k_a, r_k, *, batch, seq):
    tile = WKV_CHUNKS_PER_STEP * CHUNK
    nc = seq // tile
    z5 = z4.reshape(4, PAIRS, batch, seq, LANES)
    lora3 = lora.reshape(batch, seq, 4 * LORA)
    full = lambda shape: pl.BlockSpec(shape, lambda i: (0,) * len(shape))
    z_blk = (3, PAIRS, batch, tile, LANES)
    y_blk = (PAIRS, batch, tile, LANES)
    scr = (2, batch * WKV_CHUNKS_PER_STEP, PAIRS, CHUNK, LANES)
    yf, yr, bonus = pl.pallas_call(
        _wkv_kernel,
        grid=(nc,),
        in_specs=[
            pl.BlockSpec(z_blk, lambda i: (0, 0, 0, i, 0)),
            pl.BlockSpec(z_blk, lambda i: (0, 0, 0, nc - 1 - i, 0)),
            pl.BlockSpec((batch, tile, 4 * LORA), lambda i: (0, i, 0)),
            pl.BlockSpec((batch, tile, 4 * LORA), lambda i: (0, nc - 1 - i, 0)),
            full((2, 1, RWKV_W)), full((2, LORA, RWKV_W)),
            full((2, 1, RWKV_W)), full((2, LORA, RWKV_W)),
            full((PAIRS, 1, LANES)), full((PAIRS, 1, LANES)), full((PAIRS, 1, LANES)),
        ],
        out_specs=[
            pl.BlockSpec(y_blk, lambda i: (0, 0, i, 0)),
            pl.BlockSpec(y_blk, lambda i: (0, 0, nc - 1 - i, 0)),
            pl.BlockSpec(y_blk, lambda i: (0, 0, i, 0)),
        ],
        out_shape=[jax.ShapeDtypeStruct((PAIRS, batch, seq, LANES), F32)] * 3,
        scratch_shapes=[
            pltpu.VMEM((batch, PAIRS, 2, LANES, LANES), F32),
            pltpu.VMEM(scr, F32),
            pltpu.VMEM(scr, F32),
            pltpu.VMEM(scr, F32),
        ],
        compiler_params=pltpu.CompilerParams(
            dimension_semantics=("arbitrary",), vmem_limit_bytes=VMEM_LIMIT),
        name="wkv7",
    )(z5, z5, lora3, lora3, w0, w2, a0, a2, k_k, k_a, r_k)
    flat = lambda a: a.reshape(PAIRS, batch * seq, LANES)
    return flat(yf), flat(yr), flat(bonus)


NAT_ROWS_PER_STEP = 16
NAT_GROUP_W = 2 * LANES


def _nat_kernel(q_ref, k_ref, v_ref, g_ref, bias_ref, o_ref, *, rows):
    rb = pl.program_id(2)
    kh = min(NAT_KH, rows)
    win = kh * GRID_W
    scale = HEAD ** -0.5
    assert math.frexp(scale)[0] == 0.5, "scale is folded into bf16 q: must be a power of two"
    lane = lax.broadcasted_iota(jnp.int32, (GRID_W, LANES), 1)
    low = lane < HEAD

    def chain(qr, pp):
        r = rb * NAT_ROWS_PER_STEP + qr
        rs = jnp.clip(r - kh // 2, 0, rows - kh)
        pidx = rs - r + (NAT_KH - 1)
        q0 = qr * GRID_W
        k0 = pl.multiple_of(rs * GRID_W, GRID_W)
        sl = slice(pp * LANES, (pp + 1) * LANES)
        qp = q_ref[0, q0:q0 + GRID_W, sl]
        zq = jnp.zeros_like(qp)
        qs = jnp.concatenate([jnp.where(low, qp, zq), jnp.where(low, zq, qp)], axis=0)
        s = _dot_nt(qs, k_ref[0, pl.ds(k0, win), sl])
        yield
        s = s * scale + bias_ref[pidx, pp]
        m = jnp.max(s, axis=-1, keepdims=True)
        e = jnp.exp(s - m)
        l = jnp.sum(e, axis=-1, keepdims=True)
        o2 = _dot(e.astype(BF16), v_ref[pl.ds(k0, win), sl])
        yield
        o2 = o2 / l
        o = jnp.where(low, o2[0:GRID_W], o2[GRID_W:2 * GRID_W])
        g = g_ref[q0:q0 + GRID_W, sl]
        o_ref[q0:q0 + GRID_W, sl] = (o * (g * _sigmoid(g))).astype(o_ref.dtype)

    _round_robin([chain(qr, pp) for qr in range(NAT_ROWS_PER_STEP)
                  for pp in range(NAT_GROUP_W // LANES)])


def _nat(qk, vn, gn, bias, *, batch, seq):
    bt = gn.shape[0]
    rows = seq // GRID_W
    kh = min(NAT_KH, rows)
    tq = NAT_ROWS_PER_STEP * GRID_W
    nrb = rows // NAT_ROWS_PER_STEP
    ngrp = NAT_W // NAT_GROUP_W
    ppg = NAT_GROUP_W // LANES
    kernel = functools.partial(_nat_kernel, rows=rows)
    return pl.pallas_call(
        kernel,
        grid=(batch, ngrp, nrb),
        in_specs=[
            pl.BlockSpec((1, tq, NAT_GROUP_W), lambda b, g, r: (0, b * nrb + r, g)),
            pl.BlockSpec((1, seq, NAT_GROUP_W), lambda b, g, r: (1, b, g)),
            pl.BlockSpec((seq, NAT_GROUP_W), lambda b, g, r: (b, g)),
            pl.BlockSpec((tq, NAT_GROUP_W), lambda b, g, r: (b * nrb + r, g)),
            pl.BlockSpec((NAT_KH, ppg, 2 * GRID_W, kh * GRID_W), lambda b, g, r: (0, g, 0, 0)),
        ],
        out_specs=pl.BlockSpec((tq, NAT_GROUP_W), lambda b, g, r: (b * nrb + r, g)),
        out_shape=jax.ShapeDtypeStruct((bt, NAT_W), BF16),
        compiler_params=pltpu.CompilerParams(
            dimension_semantics=("arbitrary", "arbitrary", "arbitrary"), vmem_limit_bytes=VMEM_LIMIT),
        name="nat2d",
    )(qk, qk, vn, gn, bias)


def _nat_bias_table(rpb, rows):
    kh = min(NAT_KH, rows)
    heads, nro, nco = rpb.shape
    period = 2 * GRID_W
    u = jnp.concatenate([rpb[:, :, NAT_KW - 1:].astype(F32),
                         jnp.full((heads, nro, period - nco), NEG, F32),
                         rpb[:, :, :NAT_KW - 1].astype(F32)], axis=-1)
    flat = jnp.tile(u, (1, 1, GRID_W))[:, :, :GRID_W * (period - 1)]
    t = flat.reshape(heads, nro, GRID_W, period - 1)[:, :, :, :GRID_W]
    cols = np.arange(GRID_W)
    col_start = np.clip(cols - NAT_KW // 2, 0, GRID_W - NAT_KW)
    valid = (cols[None, :] >= col_start[:, None]) & (cols[None, :] < col_start[:, None] + NAT_KW)
    t = jnp.where(valid[None, None], t, NEG)
    t = jnp.transpose(t, (0, 2, 1, 3)).reshape(heads // 2, 2 * GRID_W, nro * GRID_W)
    return jnp.stack([t[:, :, pi * GRID_W:(pi + kh) * GRID_W] for pi in range(NAT_KH)])


def _out_kernel(yf_ref, yr_ref, bonus_ref, grw_ref, yb_ref, x_ref, p_ref, lnw_ref, lnb_ref,
                wout_ref, pg_ref, wgate_ref, wproj_ref, fg_ref, o_ref):
    r1 = lax.broadcasted_iota(jnp.int32, (LANES, LANES), 0)
    c1 = lax.broadcasted_iota(jnp.int32, (LANES, LANES), 1)
    avg_bd = jnp.where((r1 // HEAD) == (c1 // HEAD), 1.0 / HEAD, 0.0).astype(BF16)
    parts = []
    for p in range(PAIRS):
        y = yf_ref[p] + yr_ref[p]
        mu = _dot(y.astype(BF16), avg_bd)
        dlt = y - mu
        var = _dot((dlt * dlt).astype(BF16), avg_bd)
        yn = dlt * lax.rsqrt(var + LNX_EPS) * lnw_ref[p] + lnb_ref[p]
        g = grw_ref[0, p]
        parts.append(((yn + bonus_ref[p]) * (g * _sigmoid(g))).astype(BF16))
    parts.append(yb_ref[...])
    mix = jnp.concatenate(parts, axis=1)
    h = x_ref[...] + _dot(mix, wout_ref[...])

    def rms(v, gain):
        ms = jnp.mean(v * v, axis=-1, keepdims=True)
        return v * lax.rsqrt(ms + NORM_EPS) * gain

    gate = _sigmoid(_dot(rms(h, pg_ref[...]).astype(BF16), wgate_ref[...]))
    h = h + _dot(p_ref[...].astype(BF16), wproj_ref[...]) * gate
    o_ref[...] = rms(h, fg_ref[...])


def _out_stage(yf, yr, bonus, z4, yb, x2, p2, lnw, lnb, w_out, ple_g, w_gate, w_proj, final_g, *, tm):
    bt, d = x2.shape
    ple = p2.shape[1]
    const = lambda shape: pl.BlockSpec(shape, lambda i: (0,) * len(shape), pipeline_mode=pl.Buffered(1))
    pair_spec = pl.BlockSpec((PAIRS, tm, LANES), lambda i: (0, i, 0))
    return pl.pallas_call(
        _out_kernel,
        grid=(bt // tm,),
        in_specs=[
            pair_spec, pair_spec, pair_spec,
            pl.BlockSpec((1, PAIRS, tm, LANES), lambda i: (3, 0, i, 0)),
            pl.BlockSpec((tm, NAT_W), lambda i: (i, 0)),
            pl.BlockSpec((tm, d), lambda i: (i, 0)),
            pl.BlockSpec((tm, ple), lambda i: (i, 0)),
            const((PAIRS, 1, LANES)), const((PAIRS, 1, LANES)),
            const((RWKV_W + NAT_W, d)), const((1, d)), const((d, d)), const((ple, d)), const((1, d)),
        ],
        out_specs=pl.BlockSpec((tm, d), lambda i: (i, 0)),
        out_shape=jax.ShapeDtypeStruct((bt, d), F32),
        compiler_params=pltpu.CompilerParams(
            dimension_semantics=("arbitrary",), vmem_limit_bytes=VMEM_LIMIT),
        name="out_stage",
    )(yf, yr, bonus, z4, yb, x2, p2, lnw, lnb, w_out, ple_g, w_gate, w_proj, final_g)


INPROJ_TM = 512
OUT_TM = 256


def kernel(x, p, norm_mix_g, w_in, shift_mu_prev, shift_mu_next, decay_w0, decay_w2, iclr_a0, iclr_a2,
           k_k, k_a, r_k, lnx_w, lnx_b, nat_rpb, w_out, ple_norm_g, w_ple_gate, w_ple_proj, final_norm_g):
    batch, seq, d = x.shape
    depth = p.shape[0]
    bt = batch * seq
    rows = seq // GRID_W
    o_wd = 3 * RWKV_W
    shift_w = o_wd + 4 * LORA
    pairs = lambda a: a.reshape(PAIRS, 1, LANES).astype(F32)

    assert depth == 1, "the output stage fuses the final norm: one trunk layer only"
    h = x.reshape(bt, d)
    w = w_in[0]
    w_main = jnp.concatenate([w[:, :o_wd], w[:, shift_w:]], axis=1).astype(BF16)
    w_lora = w[:, o_wd:shift_w].astype(BF16)
    mup, mun = shift_mu_prev[0], shift_mu_next[0]
    z4, qk, vn, gn, lora = _inproj(
        h, norm_mix_g[0].reshape(1, d), w_main, w_lora,
        mup[:o_wd].reshape(3, 1, RWKV_W), mun[:o_wd].reshape(3, 1, RWKV_W),
        mup[o_wd:].reshape(1, 4 * LORA), mun[o_wd:].reshape(1, 4 * LORA),
        seq=seq, tm=INPROJ_TM)
    yf, yr, bonus = _wkv(
        z4, lora,
        decay_w0[0].reshape(2, 1, RWKV_W), decay_w2[0].astype(BF16),
        iclr_a0[0].reshape(2, 1, RWKV_W), iclr_a2[0].astype(BF16),
        pairs(k_k[0]), pairs(k_a[0]), pairs(r_k[0]), batch=batch, seq=seq)
    yb = _nat(qk, vn, gn, _nat_bias_table(nat_rpb[0], rows), batch=batch, seq=seq)
    out = _out_stage(
        yf, yr, bonus, z4, yb, h, p[0].reshape(bt, -1), pairs(lnx_w[0]), pairs(lnx_b[0]),
        w_out[0].astype(BF16), ple_norm_g[0].reshape(1, d), w_ple_gate[0].astype(BF16),
        w_ple_proj[0].astype(BF16), final_norm_g.reshape(1, d), tm=OUT_TM)
    return out.reshape(batch, seq, d)
```

```python
import functools
import math

import numpy as np
import jax
import jax.numpy as jnp
from jax import lax
from jax.experimental import pallas as pl
from jax.experimental.pallas import tpu as pltpu

GRID_W = 64
HEAD = 64
RWKV_W = 1024
NAT_W = 1024
LORA = 64
NAT_KH = 8
NAT_KW = 16
NORM_EPS = 1e-6
LNX_EPS = 64e-5
DECAY_SCALE = math.exp(-0.5)

LANES = 128
PACK_ROWS = 16
VMEM_LIMIT = 56 * 1024 * 1024

CHUNK = 64
SUB = CHUNK // 2
SUB_LOG2 = SUB.bit_length() - 1
assert SUB == 1 << SUB_LOG2
PAIRS = RWKV_W // LANES
WKV_CHUNKS_PER_STEP = 2
NEG = -1e30

F32 = jnp.float32
BF16 = jnp.bfloat16


def _dot(a, b):
    return jnp.dot(a, b, preferred_element_type=F32)


def _dot_nt(a, b):
    return lax.dot_general(a, b, (((1,), (1,)), ((), ())), preferred_element_type=F32)


def _dot_tn(a, b):
    return lax.dot_general(a, b, (((0,), (0,)), ((), ())), preferred_element_type=F32)


def _sigmoid(x):
    return 1.0 / (1.0 + jnp.exp(-x))


def _round_robin(chains):
    chains = list(chains)
    while chains:
        alive = []
        for c in chains:
            try:
                next(c)
                alive.append(c)
            except StopIteration:
                pass
        chains = alive


def _cast_kernel(w_ref, o_ref):
    o_ref[...] = w_ref[...].astype(o_ref.dtype)


def _drop_columns_bf16(w, lo, hi, blk):
    d, n = w.shape
    assert lo % blk == 0 and hi % blk == 0 and n % blk == 0
    first, skip = lo // blk, (hi - lo) // blk
    return pl.pallas_call(
        _cast_kernel,
        grid=(n // blk - skip,),
        in_specs=[pl.BlockSpec((d, blk), lambda j: (0, jnp.where(j >= first, j + skip, j)))],
        out_specs=pl.BlockSpec((d, blk), lambda j: (0, j)),
        out_shape=jax.ShapeDtypeStruct((d, n - (hi - lo)), BF16),
        compiler_params=pltpu.CompilerParams(dimension_semantics=("arbitrary",)),
        name="weight_regroup",
    )(w)


def _inproj_kernel(x_ref, xp_ref, xn_ref, g_ref, wa_ref, wb_ref, wl_ref, mup_ref, mun_ref, mupl_ref,
                   munl_ref, z4_ref, qk_ref, vn_ref, gn_ref, lora_ref, hn_ref, *, tm, tiles_per_batch):
    i = pl.program_id(0)
    j = pl.program_id(1)
    halo = PACK_ROWS
    rows = tm + 2 * halo

    def shifted(z, mup, mun):
        zp = pltpu.roll(z, 1, 0)[halo:halo + tm]
        zn = pltpu.roll(z, rows - 1, 0)[halo:halo + tm]
        zc = z[halo:halo + tm]
        return zc + mup * (zp - zc) + mun * (zn - zc)

    def store_pairs(slot, z):
        for p in range(PAIRS):
            z4_ref[slot, p] = z[:, p * LANES:(p + 1) * LANES]

    def centre(w_ref):
        return _dot(hn_ref[halo:halo + tm], w_ref[...])

    @pl.when(j == 0)
    def _():
        g = g_ref[...]

        def norm(xv):
            ms = jnp.mean(xv * xv, axis=-1, keepdims=True)
            return xv * lax.rsqrt(ms + NORM_EPS) * g

        ti = i % tiles_per_batch
        keep_p = (ti > 0).astype(F32)
        keep_n = (ti < tiles_per_batch - 1).astype(F32)
        hn_ref[0:halo] = (norm(xp_ref[...]) * keep_p).astype(BF16)
        hn_ref[halo:halo + tm] = norm(x_ref[...]).astype(BF16)
        hn_ref[halo + tm:rows] = (norm(xn_ref[...]) * keep_n).astype(BF16)
        zl = _dot(hn_ref[...], wl_ref[...])
        lora_ref[...] = shifted(zl, mupl_ref[...], munl_ref[...])
        store_pairs(0, shifted(_dot(hn_ref[...], wa_ref[...]), mup_ref[0], mun_ref[0]))
        store_pairs(1, shifted(_dot(hn_ref[...], wb_ref[...]), mup_ref[1], mun_ref[1]))

    @pl.when(j == 1)
    def _():
        store_pairs(0, shifted(_dot(hn_ref[...], wa_ref[...]), mup_ref[2], mun_ref[2]))
        store_pairs(1, centre(wb_ref))

    @pl.when(j == 2)
    def _():
        qk_ref[0] = centre(wa_ref).astype(BF16)
        qk_ref[1] = centre(wb_ref).astype(BF16)

    @pl.when(j == 3)
    def _():
        vn_ref[...] = centre(wa_ref).astype(BF16)
        gn_ref[...] = centre(wb_ref)


def _inproj(x2, norm_g, w_main, w_lora, mup, mun, mupl, munl, *, seq, tm):
    bt, d = x2.shape
    halo = PACK_ROWS
    tiles_per_batch = seq // tm
    n_i = bt // tm
    hb = tm // halo
    last_hb = bt // halo - 1
    n_j = w_main.shape[1] // (2 * RWKV_W)
    const = lambda shape: pl.BlockSpec(shape, lambda i, j: (0,) * len(shape))
    kernel = functools.partial(_inproj_kernel, tm=tm, tiles_per_batch=tiles_per_batch)
    return pl.pallas_call(
        kernel,
        grid=(n_i, n_j),
        in_specs=[
            pl.BlockSpec((tm, d), lambda i, j: (i, 0)),
            pl.BlockSpec((halo, d), lambda i, j: (jnp.maximum(i * hb - 1, 0), 0)),
            pl.BlockSpec((halo, d), lambda i, j: (jnp.minimum((i + 1) * hb, last_hb), 0)),
            const((1, d)),
            pl.BlockSpec((d, RWKV_W), lambda i, j: (0, 2 * j)),
            pl.BlockSpec((d, RWKV_W), lambda i, j: (0, 2 * j + 1)),
            const((d, 4 * LORA)),
            const((3, 1, RWKV_W)), const((3, 1, RWKV_W)),
            const((1, 4 * LORA)), const((1, 4 * LORA)),
        ],
        out_specs=[
            pl.BlockSpec((2, PAIRS, tm, LANES), lambda i, j: (jnp.minimum(j, 1), 0, i, 0)),
            pl.BlockSpec((2, tm, NAT_W), lambda i, j: (0, i, 0)),
            pl.BlockSpec((tm, NAT_W), lambda i, j: (i, 0)),
            pl.BlockSpec((tm, NAT_W), lambda i, j: (i, 0)),
            pl.BlockSpec((tm, 4 * LORA), lambda i, j: (i, 0)),
        ],
        out_shape=[
            jax.ShapeDtypeStruct((4, PAIRS, bt, LANES), F32),
            jax.ShapeDtypeStruct((2, bt, NAT_W), BF16),
            jax.ShapeDtypeStruct((bt, NAT_W), BF16),
            jax.ShapeDtypeStruct((bt, NAT_W), F32),
            jax.ShapeDtypeStruct((bt, 4 * LORA), F32),
        ],
        scratch_shapes=[pltpu.VMEM((tm + 2 * halo, d), BF16)],
        compiler_params=pltpu.CompilerParams(
            dimension_semantics=("arbitrary", "arbitrary"), vmem_limit_bytes=VMEM_LIMIT),
        name="inproj",
    )(x2, x2, x2, norm_g, w_main, w_main, w_lora, mup, mun, mupl, munl)


def _wkv_kernel(zf_ref, zr_ref, lf_ref, lr_ref, w0_ref, w2_ref, a0_ref, a2_ref, kk_ref, ka_ref, rk_ref,
                yf_ref, yr_ref, bonus_ref, st_ref, cinc_ref, lw_ref, iclr_ref):
    L = CHUNK
    W4 = 2 * LANES
    nb = lf_ref.shape[0]
    nsub = lf_ref.shape[1] // L
    nblk = nb * nsub

    @pl.when(pl.program_id(0) == 0)
    def _():
        st_ref[...] = jnp.zeros_like(st_ref)

    row = lax.broadcasted_iota(jnp.int32, (nblk * L, nblk * L), 0)
    col = lax.broadcasted_iota(jnp.int32, (nblk * L, nblk * L), 1)
    same = (row // L) == (col // L)
    for d, l_ref in ((0, lf_ref), (1, lr_ref)):
        lo = l_ref[...].reshape(nblk * L, 4 * LORA)
        wd = lo[:, d * LORA:(d + 1) * LORA]
        ad = lo[:, (2 + d) * LORA:(3 + d) * LORA]
        dw = _dot(jnp.tanh(wd).astype(BF16), w2_ref[d])
        lw = -DECAY_SCALE * _sigmoid(w0_ref[d] + dw)
        da = _dot(ad.astype(BF16), a2_ref[d])
        ic = _sigmoid(a0_ref[d] + da)
        tri = (same & ((row >= col) if d == 0 else (col >= row))).astype(BF16)
        hi = lw.astype(BF16)
        lo2 = (lw - hi.astype(F32)).astype(BF16)
        cinc = _dot(tri, hi) + _dot(tri, lo2)
        for q in range(nblk):
            for p in range(PAIRS):
                sl = slice(p * LANES, (p + 1) * LANES)
                cinc_ref[d, q, p] = cinc[q * L:(q + 1) * L, sl]
                lw_ref[d, q, p] = lw[q * L:(q + 1) * L, sl]
                iclr_ref[d, q, p] = ic[q * L:(q + 1) * L, sl]

    ri = lax.broadcasted_iota(jnp.int32, (L, W4), 0)
    li = lax.broadcasted_iota(jnp.int32, (L, W4), 1)
    blk = li // HEAD
    jj = li % HEAD
    fwd = li < LANES
    strict = (fwd & (jj < ri)) | (~fwd & (jj > ri))
    incl = (fwd & (jj <= ri)) | (~fwd & (jj >= ri))
    eye = (jj == ri).astype(F32)
    diag_blk = (jj // SUB) == (ri // SUB)
    r1 = lax.broadcasted_iota(jnp.int32, (LANES, LANES), 0)
    c1 = lax.broadcasted_iota(jnp.int32, (LANES, LANES), 1)
    ones_bd_mask = (r1 // HEAD) == (c1 // HEAD)
    ones_bd = ones_bd_mask.astype(BF16)

    def bd(x):
        xb = x.astype(BF16)
        z = jnp.zeros_like(xb)
        return jnp.concatenate([jnp.where(blk == q, xb, z) for q in range(4)], axis=0)

    def mm(a, b):
        return _dot(a.astype(BF16), bd(b))

    state_written = set()

    def pair_chain(b, p, c):
        half = (c, nsub - 1 - c)
        rows = [slice(h * L, (h + 1) * L) for h in half]
        zf = lambda i: zf_ref[i, p, b, rows[0]]
        zr = lambda i: zr_ref[i, p, b, rows[1]]
        kkp = kk_ref[p]
        kap = ka_ref[p]
        rkp = rk_ref[p]
        sums = []
        for z in (zf, zr):
            kraw = z(1) * kkp
            sums.append(_dot((kraw * kraw).astype(BF16), ones_bd))
        rk = _dot((zf(0) * zf(1) * rkp).astype(BF16), ones_bd)
        yield
        bonus_ref[p, b, rows[0]] = rk * zf(2)
        rt, at, bt, kt, bh, kh, vv, pl_tot = [], [], [], [], [], [], [], []
        for d, z in ((0, zf), (1, zr)):
            r = z(0)
            k = z(1)
            q = b * nsub + half[d]
            ci = cinc_ref[d, q, p]
            lw = lw_ref[d, q, p]
            ic = iclr_ref[d, q, p]
            kk = (k * kkp) * lax.rsqrt(jnp.maximum(sums[d], 1e-24))
            kd = k * (1.0 + (ic - 1.0) * kap)
            kb = kk * ic
            ctot = ci[L - 1:L] if d == 0 else ci[0:1]
            pinv = jnp.exp(-ci)
            ptot = jnp.exp(ctot)
            bti = kb * pinv
            kti = kd * pinv
            rt.append((r * jnp.exp(ci)).astype(BF16))
            at.append((-kk * jnp.exp(ci - lw)).astype(BF16))
            bt.append(bti.astype(BF16))
            kt.append(kti.astype(BF16))
            bh.append((bti * ptot).astype(BF16))
            kh.append((kti * ptot).astype(BF16))
            vv.append(z(2).astype(BF16))
            pl_tot.append(ptot)

        cat = lambda xs: jnp.concatenate(xs, axis=1)
        rt, at, bt, kt, bh, kh, vv, pl_tot = map(cat, (rt, at, bt, kt, bh, kh, vv, pl_tot))
        bhkh = jnp.concatenate([bh, kh], axis=0)

        o1 = _dot(jnp.concatenate([at, rt], axis=0),
                  jnp.concatenate([bd(bt).T, bd(kt).T], axis=1))
        yield
        a_ab = jnp.where(strict, o1[0:L, 0:W4], 0.0)
        a_ak = jnp.where(strict, o1[0:L, W4:2 * W4], 0.0)
        a_r = jnp.concatenate([jnp.where(incl, o1[L:2 * L, 0:W4], 0.0),
                               jnp.where(incl, o1[L:2 * L, W4:2 * W4], 0.0)], axis=1).astype(BF16)

        dg = jnp.where(diag_blk, a_ab, 0.0)
        off = (a_ab - dg).astype(BF16)
        tn = eye + dg
        pw = mm(dg, dg)
        av = mm(a_ak, vv)
        yield
        for _ in range(SUB_LOG2 - 2):
            o = mm(jnp.concatenate([tn, pw], axis=0), pw)
            yield
            tn = tn + o[0:L]
            pw = o[L:2 * L]
        o = mm(tn, pw)
        yield
        td = tn + o
        f = mm(td, off)
        while c > 0 and (b, p, c - 1) not in state_written:
            yield
        ar = jnp.concatenate([at, rt], axis=0)
        o5 = jnp.concatenate(
            [_dot_nt(ar[:, d * LANES:(d + 1) * LANES], st_ref[b, p, d].astype(BF16))
             for d in range(2)], axis=1)
        yield
        o = mm(f, td)
        yield
        tinv = td + o
        o = mm(tinv, o5[0:L] + av)
        yield
        u = o.astype(BF16)
        o6 = _dot(a_r, jnp.concatenate([bd(u), bd(vv)], axis=0))
        uvv = jnp.concatenate([u, vv], axis=0)
        gs = [_dot_tn(uvv[:, d * LANES:(d + 1) * LANES], bhkh[:, d * LANES:(d + 1) * LANES])
              for d in range(2)]
        yield
        y = o5[L:2 * L] + o6
        for d in range(2):
            st_ref[b, p, d] = (st_ref[b, p, d] * pl_tot[:, d * LANES:(d + 1) * LANES]
                               + jnp.where(ones_bd_mask, gs[d], 0.0))
        state_written.add((b, p, c))
        yf_ref[p, b, rows[0]] = y[:, 0:LANES]
        yr_ref[p, b, rows[1]] = y[:, LANES:2 * LANES]

    _round_robin([pair_chain(b, p, c) for c in range(nsub) for b in range(nb) for p in range(PAIRS)])


def _wkv(z4, lora, w0, w2, a0, a2, k_k, k_a, r_k, *, batch, seq):
    tile = WKV_CHUNKS_PER_STEP * CHUNK
    nc = seq // tile
    z5 = z4.reshape(4, PAIRS, batch, seq, LANES)
    lora3 = lora.reshape(batch, seq, 4 * LORA)
    full = lambda shape: pl.BlockSpec(shape, lambda i: (0,) * len(shape))
    z_blk = (3, PAIRS, batch, tile, LANES)
    y_blk = (PAIRS, batch, tile, LANES)
    scr = (2, batch * WKV_CHUNKS_PER_STEP, PAIRS, CHUNK, LANES)
    yf, yr, bonus = pl.pallas_call(
        _wkv_kernel,
        grid=(nc,),
        in_specs=[
            pl.BlockSpec(z_blk, lambda i: (0, 0, 0, i, 0)),
            pl.BlockSpec(z_blk, lambda i: (0, 0, 0, nc - 1 - i, 0)),
            pl.BlockSpec((batch, tile, 4 * LORA), lambda i: (0, i, 0)),
            pl.BlockSpec((batch, tile, 4 * LORA), lambda i: (0, nc - 1 - i, 0)),
            full((2, 1, RWKV_W)), full((2, LORA, RWKV_W)),
            full((2, 1, RWKV_W)), full((2, LORA, RWKV_W)),
            full((PAIRS, 1, LANES)), full((PAIRS, 1, LANES)), full((PAIRS, 1, LANES)),
        ],
        out_specs=[
            pl.BlockSpec(y_blk, lambda i: (0, 0, i, 0)),
            pl.BlockSpec(y_blk, lambda i: (0, 0, nc - 1 - i, 0)),
            pl.BlockSpec(y_blk, lambda i: (0, 0, i, 0)),
        ],
        out_shape=[jax.ShapeDtypeStruct((PAIRS, batch, seq, LANES), F32)] * 3,
        scratch_shapes=[
            pltpu.VMEM((batch, PAIRS, 2, LANES, LANES), F32),
            pltpu.VMEM(scr, F32),
            pltpu.VMEM(scr, F32),
            pltpu.VMEM(scr, F32),
        ],
        compiler_params=pltpu.CompilerParams(
            dimension_semantics=("arbitrary",), vmem_limit_bytes=VMEM_LIMIT),
        name="wkv7",
    )(z5, z5, lora3, lora3, w0, w2, a0, a2, k_k, k_a, r_k)
    flat = lambda a: a.reshape(PAIRS, batch * seq, LANES)
    return flat(yf), flat(yr), flat(bonus)


NAT_ROWS_PER_STEP = 16
NAT_GROUP_W = 2 * LANES


def _nat_kernel(q_ref, k_ref, v_ref, g_ref, bias_ref, o_ref, *, rows):
    rb = pl.program_id(2)
    kh = min(NAT_KH, rows)
    win = kh * GRID_W
    scale = HEAD ** -0.5
    assert math.frexp(scale)[0] == 0.5, "scale is folded into bf16 q: must be a power of two"
    lane = lax.broadcasted_iota(jnp.int32, (GRID_W, LANES), 1)
    low = lane < HEAD

    def chain(qr, pp):
        r = rb * NAT_ROWS_PER_STEP + qr
        rs = jnp.clip(r - kh // 2, 0, rows - kh)
        pidx = rs - r + (NAT_KH - 1)
        q0 = qr * GRID_W
        k0 = pl.multiple_of(rs * GRID_W, GRID_W)
        sl = slice(pp * LANES, (pp + 1) * LANES)
        qp = q_ref[0, q0:q0 + GRID_W, sl]
        zq = jnp.zeros_like(qp)
        qs = jnp.concatenate([jnp.where(low, qp, zq), jnp.where(low, zq, qp)], axis=0)
        s = _dot_nt(qs, k_ref[0, pl.ds(k0, win), sl])
        yield
        s = s * scale + bias_ref[pidx, pp]
        m = jnp.max(s, axis=-1, keepdims=True)
        e = jnp.exp(s - m)
        l = jnp.sum(e, axis=-1, keepdims=True)
        o2 = _dot(e.astype(BF16), v_ref[pl.ds(k0, win), sl])
        yield
        o2 = o2 / l
        o = jnp.where(low, o2[0:GRID_W], o2[GRID_W:2 * GRID_W])
        g = g_ref[q0:q0 + GRID_W, sl]
        o_ref[q0:q0 + GRID_W, sl] = (o * (g * _sigmoid(g))).astype(o_ref.dtype)

    _round_robin([chain(qr, pp) for qr in range(NAT_ROWS_PER_STEP)
                  for pp in range(NAT_GROUP_W // LANES)])


def _nat(qk, vn, gn, bias, *, batch, seq):
    bt = gn.shape[0]
    rows = seq // GRID_W
    kh = min(NAT_KH, rows)
    tq = NAT_ROWS_PER_STEP * GRID_W
    nrb = rows // NAT_ROWS_PER_STEP
    ngrp = NAT_W // NAT_GROUP_W
    ppg = NAT_GROUP_W // LANES
    kernel = functools.partial(_nat_kernel, rows=rows)
    return pl.pallas_call(
        kernel,
        grid=(batch, ngrp, nrb),
        in_specs=[
            pl.BlockSpec((1, tq, NAT_GROUP_W), lambda b, g, r: (0, b * nrb + r, g)),
            pl.BlockSpec((1, seq, NAT_GROUP_W), lambda b, g, r: (1, b, g)),
            pl.BlockSpec((seq, NAT_GROUP_W), lambda b, g, r: (b, g)),
            pl.BlockSpec((tq, NAT_GROUP_W), lambda b, g, r: (b * nrb + r, g)),
            pl.BlockSpec((NAT_KH, ppg, 2 * GRID_W, kh * GRID_W), lambda b, g, r: (0, g, 0, 0)),
        ],
        out_specs=pl.BlockSpec((tq, NAT_GROUP_W), lambda b, g, r: (b * nrb + r, g)),
        out_shape=jax.ShapeDtypeStruct((bt, NAT_W), BF16),
        compiler_params=pltpu.CompilerParams(
            dimension_semantics=("arbitrary", "arbitrary", "arbitrary"), vmem_limit_bytes=VMEM_LIMIT),
        name="nat2d",
    )(qk, qk, vn, gn, bias)


def _nat_bias_table(rpb, rows):
    kh = min(NAT_KH, rows)
    heads, nro, nco = rpb.shape
    cols = np.arange(GRID_W)
    col_start = np.clip(cols - NAT_KW // 2, 0, GRID_W - NAT_KW)
    valid = (cols[None, :] >= col_start[:, None]) & (cols[None, :] < col_start[:, None] + NAT_KW)
    rel = cols[None, :] - cols[:, None] + NAT_KW - 1
    onehot = ((rel[None] == np.arange(nco)[:, None, None]) & valid[None]).astype(np.float32)
    t = jnp.einsum('hrd,dck->hcrk', rpb.astype(F32), onehot, precision=lax.Precision.HIGHEST)
    t = jnp.where(valid[None, :, None, :], t, NEG)
    t = t.reshape(heads // 2, 2 * GRID_W, nro * GRID_W)
    return jnp.stack([t[:, :, pi * GRID_W:(pi + kh) * GRID_W] for pi in range(NAT_KH)])


def _out_kernel(yf_ref, yr_ref, bonus_ref, grw_ref, yb_ref, x_ref, p_ref, lnw_ref, lnb_ref,
                wout_ref, pg_ref, wgate_ref, wproj_ref, fg_ref, o_ref):
    r1 = lax.broadcasted_iota(jnp.int32, (LANES, LANES), 0)
    c1 = lax.broadcasted_iota(jnp.int32, (LANES, LANES), 1)
    avg_bd = jnp.where((r1 // HEAD) == (c1 // HEAD), 1.0 / HEAD, 0.0).astype(BF16)
    parts = []
    for p in range(PAIRS):
        y = yf_ref[p] + yr_ref[p]
        mu = _dot(y.astype(BF16), avg_bd)
        dlt = y - mu
        var = _dot((dlt * dlt).astype(BF16), avg_bd)
        yn = dlt * lax.rsqrt(var + LNX_EPS) * lnw_ref[p] + lnb_ref[p]
        g = grw_ref[0, p]
        parts.append(((yn + bonus_ref[p]) * (g * _sigmoid(g))).astype(BF16))
    parts.append(yb_ref[...])
    mix = jnp.concatenate(parts, axis=1)
    h = x_ref[...] + _dot(mix, wout_ref[...])

    def rms(v, gain):
        ms = jnp.mean(v * v, axis=-1, keepdims=True)
        return v * lax.rsqrt(ms + NORM_EPS) * gain

    gate = _sigmoid(_dot(rms(h, pg_ref[...]).astype(BF16), wgate_ref[...]))
    h = h + _dot(p_ref[...].astype(BF16), wproj_ref[...]) * gate
    o_ref[...] = rms(h, fg_ref[...])


def _out_stage(yf, yr, bonus, z4, yb, x2, p2, lnw, lnb, w_out, ple_g, w_gate, w_proj, final_g, *, tm):
    bt, d = x2.shape
    ple = p2.shape[1]
    const = lambda shape: pl.BlockSpec(shape, lambda i: (0,) * len(shape), pipeline_mode=pl.Buffered(1))
    pair_spec = pl.BlockSpec((PAIRS, tm, LANES), lambda i: (0, i, 0))
    return pl.pallas_call(
        _out_kernel,
        grid=(bt // tm,),
        in_specs=[
            pair_spec, pair_spec, pair_spec,
            pl.BlockSpec((1, PAIRS, tm, LANES), lambda i: (3, 0, i, 0)),
            pl.BlockSpec((tm, NAT_W), lambda i: (i, 0)),
            pl.BlockSpec((tm, d), lambda i: (i, 0)),
            pl.BlockSpec((tm, ple), lambda i: (i, 0)),
            const((PAIRS, 1, LANES)), const((PAIRS, 1, LANES)),
            const((RWKV_W + NAT_W, d)), const((1, d)), const((d, d)), const((ple, d)), const((1, d)),
        ],
        out_specs=pl.BlockSpec((tm, d), lambda i: (i, 0)),
        out_shape=jax.ShapeDtypeStruct((bt, d), F32),
        compiler_params=pltpu.CompilerParams(
            dimension_semantics=("arbitrary",), vmem_limit_bytes=VMEM_LIMIT),
        name="out_stage",
    )(yf, yr, bonus, z4, yb, x2, p2, lnw, lnb, w_out, ple_g, w_gate, w_proj, final_g)


INPROJ_TM = 512
OUT_TM = 256


def kernel(x, p, norm_mix_g, w_in, shift_mu_prev, shift_mu_next, decay_w0, decay_w2, iclr_a0, iclr_a2,
           k_k, k_a, r_k, lnx_w, lnx_b, nat_rpb, w_out, ple_norm_g, w_ple_gate, w_ple_proj, final_norm_g):
    batch, seq, d = x.shape
    depth = p.shape[0]
    bt = batch * seq
    rows = seq // GRID_W
    o_wd = 3 * RWKV_W
    shift_w = o_wd + 4 * LORA
    pairs = lambda a: a.reshape(PAIRS, 1, LANES).astype(F32)

    assert depth == 1, "the output stage fuses the final norm: one trunk layer only"
    h = x.reshape(bt, d)
    w = w_in[0]
    w_main = _drop_columns_bf16(w, o_wd, shift_w, 4 * LORA)
    w_lora = w[:, o_wd:shift_w].astype(BF16)
    mup, mun = shift_mu_prev[0], shift_mu_next[0]
    z4, qk, vn, gn, lora = _inproj(
        h, norm_mix_g[0].reshape(1, d), w_main, w_lora,
        mup[:o_wd].reshape(3, 1, RWKV_W), mun[:o_wd].reshape(3, 1, RWKV_W),
        mup[o_wd:].reshape(1, 4 * LORA), mun[o_wd:].reshape(1, 4 * LORA),
        seq=seq, tm=INPROJ_TM)
    yf, yr, bonus = _wkv(
        z4, lora,
        decay_w0[0].reshape(2, 1, RWKV_W), decay_w2[0].astype(BF16),
        iclr_a0[0].reshape(2, 1, RWKV_W), iclr_a2[0].astype(BF16),
        pairs(k_k[0]), pairs(k_a[0]), pairs(r_k[0]), batch=batch, seq=seq)
    yb = _nat(qk, vn, gn, _nat_bias_table(nat_rpb[0], rows), batch=batch, seq=seq)
    out = _out_stage(
        yf, yr, bonus, z4, yb, h, p[0].reshape(bt, -1), pairs(lnx_w[0]), pairs(lnx_b[0]),
        w_out[0].astype(BF16), ple_norm_g[0].reshape(1, d), w_ple_gate[0].astype(BF16),
        w_ple_proj[0].astype(BF16), final_norm_g.reshape(1, d), tm=OUT_TM)
    return out.reshape(batch, seq, d)
```

```python
import functools
import math

import numpy as np
import jax
import jax.numpy as jnp
from jax import lax
from jax.experimental import pallas as pl
from jax.experimental.pallas import tpu as pltpu

GRID_W = 64
HEAD = 64
RWKV_W = 1024
NAT_W = 1024
LORA = 64
NAT_KH = 8
NAT_KW = 16
NORM_EPS = 1e-6
LNX_EPS = 64e-5
DECAY_SCALE = math.exp(-0.5)

LANES = 128
PACK_ROWS = 16
VMEM_LIMIT = 56 * 1024 * 1024

CHUNK = 64
SUB = CHUNK // 2
SUB_LOG2 = SUB.bit_length() - 1
assert SUB == 1 << SUB_LOG2
PAIRS = RWKV_W // LANES
WKV_CHUNKS_PER_STEP = 2
NEG = -1e30

F32 = jnp.float32
BF16 = jnp.bfloat16


def _dot(a, b):
    return jnp.dot(a, b, preferred_element_type=F32)


def _dot_nt(a, b):
    return lax.dot_general(a, b, (((1,), (1,)), ((), ())), preferred_element_type=F32)


def _dot_tn(a, b):
    return lax.dot_general(a, b, (((0,), (0,)), ((), ())), preferred_element_type=F32)


def _sigmoid(x):
    return 1.0 / (1.0 + jnp.exp(-x))


def _round_robin(chains):
    chains = list(chains)
    while chains:
        alive = []
        for c in chains:
            try:
                next(c)
                alive.append(c)
            except StopIteration:
                pass
        chains = alive


def _cast_kernel(w_ref, o_ref):
    o_ref[...] = w_ref[...].astype(o_ref.dtype)


def _drop_columns_bf16(w, lo, hi, blk):
    d, n = w.shape
    assert lo % blk == 0 and hi % blk == 0 and n % blk == 0
    first, skip = lo // blk, (hi - lo) // blk
    return pl.pallas_call(
        _cast_kernel,
        grid=(n // blk - skip,),
        in_specs=[pl.BlockSpec((d, blk), lambda j: (0, jnp.where(j >= first, j + skip, j)))],
        out_specs=pl.BlockSpec((d, blk), lambda j: (0, j)),
        out_shape=jax.ShapeDtypeStruct((d, n - (hi - lo)), BF16),
        compiler_params=pltpu.CompilerParams(dimension_semantics=("arbitrary",)),
        name="weight_regroup",
    )(w)


def _inproj_kernel(x_ref, xp_ref, xn_ref, g_ref, wa_ref, wb_ref, wl_ref, mup_ref, mun_ref, mupl_ref,
                   munl_ref, z4_ref, qk_ref, vn_ref, gn_ref, lora_ref, hn_ref, *, tm, tiles_per_batch):
    i = pl.program_id(0)
    j = pl.program_id(1)
    halo = PACK_ROWS
    rows = tm + 2 * halo

    def shifted(z, mup, mun):
        zp = pltpu.roll(z, 1, 0)[halo:halo + tm]
        zn = pltpu.roll(z, rows - 1, 0)[halo:halo + tm]
        zc = z[halo:halo + tm]
        return zc + mup * (zp - zc) + mun * (zn - zc)

    def store_pairs(slot, z):
        for p in range(PAIRS):
            z4_ref[slot, p] = z[:, p * LANES:(p + 1) * LANES]

    def centre(w_ref):
        return _dot(hn_ref[halo:halo + tm], w_ref[...])

    @pl.when(j == 0)
    def _():
        g = g_ref[...]

        def norm(xv):
            ms = jnp.mean(xv * xv, axis=-1, keepdims=True)
            return xv * lax.rsqrt(ms + NORM_EPS) * g

        ti = i % tiles_per_batch
        keep_p = (ti > 0).astype(F32)
        keep_n = (ti < tiles_per_batch - 1).astype(F32)
        hn_ref[0:halo] = (norm(xp_ref[...]) * keep_p).astype(BF16)
        hn_ref[halo:halo + tm] = norm(x_ref[...]).astype(BF16)
        hn_ref[halo + tm:rows] = (norm(xn_ref[...]) * keep_n).astype(BF16)
        zl = _dot(hn_ref[...], wl_ref[...])
        lora_ref[...] = shifted(zl, mupl_ref[...], munl_ref[...])
        store_pairs(0, shifted(_dot(hn_ref[...], wa_ref[...]), mup_ref[0], mun_ref[0]))
        store_pairs(1, shifted(_dot(hn_ref[...], wb_ref[...]), mup_ref[1], mun_ref[1]))

    @pl.when(j == 1)
    def _():
        store_pairs(0, shifted(_dot(hn_ref[...], wa_ref[...]), mup_ref[2], mun_ref[2]))
        store_pairs(1, centre(wb_ref))

    @pl.when(j == 2)
    def _():
        qk_ref[0] = centre(wa_ref).astype(BF16)
        qk_ref[1] = centre(wb_ref).astype(BF16)

    @pl.when(j == 3)
    def _():
        vn_ref[...] = centre(wa_ref).astype(BF16)
        gn_ref[...] = centre(wb_ref)


def _inproj(x2, norm_g, w_main, w_lora, mup, mun, mupl, munl, *, seq, tm):
    bt, d = x2.shape
    halo = PACK_ROWS
    tiles_per_batch = seq // tm
    n_i = bt // tm
    hb = tm // halo
    last_hb = bt // halo - 1
    n_j = w_main.shape[1] // (2 * RWKV_W)
    const = lambda shape: pl.BlockSpec(shape, lambda i, j: (0,) * len(shape))
    kernel = functools.partial(_inproj_kernel, tm=tm, tiles_per_batch=tiles_per_batch)
    return pl.pallas_call(
        kernel,
        grid=(n_i, n_j),
        in_specs=[
            pl.BlockSpec((tm, d), lambda i, j: (i, 0)),
            pl.BlockSpec((halo, d), lambda i, j: (jnp.maximum(i * hb - 1, 0), 0)),
            pl.BlockSpec((halo, d), lambda i, j: (jnp.minimum((i + 1) * hb, last_hb), 0)),
            const((1, d)),
            pl.BlockSpec((d, RWKV_W), lambda i, j: (0, 2 * j)),
            pl.BlockSpec((d, RWKV_W), lambda i, j: (0, 2 * j + 1)),
            const((d, 4 * LORA)),
            const((3, 1, RWKV_W)), const((3, 1, RWKV_W)),
            const((1, 4 * LORA)), const((1, 4 * LORA)),
        ],
        out_specs=[
            pl.BlockSpec((2, PAIRS, tm, LANES), lambda i, j: (jnp.minimum(j, 1), 0, i, 0)),
            pl.BlockSpec((2, tm, NAT_W), lambda i, j: (0, i, 0)),
            pl.BlockSpec((tm, NAT_W), lambda i, j: (i, 0)),
            pl.BlockSpec((tm, NAT_W), lambda i, j: (i, 0)),
            pl.BlockSpec((tm, 4 * LORA), lambda i, j: (i, 0)),
        ],
        out_shape=[
            jax.ShapeDtypeStruct((4, PAIRS, bt, LANES), F32),
            jax.ShapeDtypeStruct((2, bt, NAT_W), BF16),
            jax.ShapeDtypeStruct((bt, NAT_W), BF16),
            jax.ShapeDtypeStruct((bt, NAT_W), F32),
            jax.ShapeDtypeStruct((bt, 4 * LORA), F32),
        ],
        scratch_shapes=[pltpu.VMEM((tm + 2 * halo, d), BF16)],
        compiler_params=pltpu.CompilerParams(
            dimension_semantics=("arbitrary", "arbitrary"), vmem_limit_bytes=VMEM_LIMIT),
        name="inproj",
    )(x2, x2, x2, norm_g, w_main, w_main, w_lora, mup, mun, mupl, munl)


def _wkv_kernel(zf_ref, zr_ref, lf_ref, lr_ref, w0_ref, w2_ref, a0_ref, a2_ref, kk_ref, ka_ref, rk_ref,
                yf_ref, yr_ref, bonus_ref, st_ref, cinc_ref, lw_ref, iclr_ref):
    L = CHUNK
    W4 = 2 * LANES
    nb = lf_ref.shape[0]
    nsub = lf_ref.shape[1] // L
    nblk = nb * nsub

    @pl.when(pl.program_id(0) == 0)
    def _():
        st_ref[...] = jnp.zeros_like(st_ref)

    row = lax.broadcasted_iota(jnp.int32, (nblk * L, nblk * L), 0)
    col = lax.broadcasted_iota(jnp.int32, (nblk * L, nblk * L), 1)
    same = (row // L) == (col // L)
    for d, l_ref in ((0, lf_ref), (1, lr_ref)):
        lo = l_ref[...].reshape(nblk * L, 4 * LORA)
        wd = lo[:, d * LORA:(d + 1) * LORA]
        ad = lo[:, (2 + d) * LORA:(3 + d) * LORA]
        dw = _dot(jnp.tanh(wd).astype(BF16), w2_ref[d])
        lw = -DECAY_SCALE * _sigmoid(w0_ref[d] + dw)
        da = _dot(ad.astype(BF16), a2_ref[d])
        ic = _sigmoid(a0_ref[d] + da)
        tri = (same & ((row >= col) if d == 0 else (col >= row))).astype(BF16)
        hi = lw.astype(BF16)
        lo2 = (lw - hi.astype(F32)).astype(BF16)
        cinc = _dot(tri, hi) + _dot(tri, lo2)
        for q in range(nblk):
            for p in range(PAIRS):
                sl = slice(p * LANES, (p + 1) * LANES)
                cinc_ref[d, q, p] = cinc[q * L:(q + 1) * L, sl]
                lw_ref[d, q, p] = lw[q * L:(q + 1) * L, sl]
                iclr_ref[d, q, p] = ic[q * L:(q + 1) * L, sl]

    ri = lax.broadcasted_iota(jnp.int32, (L, W4), 0)
    li = lax.broadcasted_iota(jnp.int32, (L, W4), 1)
    blk = li // HEAD
    jj = li % HEAD
    fwd = li < LANES
    strict = (fwd & (jj < ri)) | (~fwd & (jj > ri))
    incl = (fwd & (jj <= ri)) | (~fwd & (jj >= ri))
    eye = (jj == ri).astype(F32)
    diag_blk = (jj // SUB) == (ri // SUB)
    r1 = lax.broadcasted_iota(jnp.int32, (LANES, LANES), 0)
    c1 = lax.broadcasted_iota(jnp.int32, (LANES, LANES), 1)
    ones_bd_mask = (r1 // HEAD) == (c1 // HEAD)
    ones_bd = ones_bd_mask.astype(BF16)

    def bd(x):
        xb = x.astype(BF16)
        z = jnp.zeros_like(xb)
        return jnp.concatenate([jnp.where(blk == q, xb, z) for q in range(4)], axis=0)

    def mm(a, b):
        return _dot(a.astype(BF16), bd(b))

    state_written = set()

    def pair_chain(b, p, c):
        half = (c, nsub - 1 - c)
        rows = [slice(h * L, (h + 1) * L) for h in half]
        zf = lambda i: zf_ref[i, p, b, rows[0]]
        zr = lambda i: zr_ref[i, p, b, rows[1]]
        kkp = kk_ref[p]
        kap = ka_ref[p]
        rkp = rk_ref[p]
        sums = []
        for z in (zf, zr):
            kraw = z(1) * kkp
            sums.append(_dot((kraw * kraw).astype(BF16), ones_bd))
        rk = _dot((zf(0) * zf(1) * rkp).astype(BF16), ones_bd)
        yield
        bonus_ref[p, b, rows[0]] = rk * zf(2)
        rt, at, bt, kt, bh, kh, vv, pl_tot = [], [], [], [], [], [], [], []
        for d, z in ((0, zf), (1, zr)):
            r = z(0)
            k = z(1)
            q = b * nsub + half[d]
            ci = cinc_ref[d, q, p]
            lw = lw_ref[d, q, p]
            ic = iclr_ref[d, q, p]
            kk = (k * kkp) * lax.rsqrt(jnp.maximum(sums[d], 1e-24))
            kd = k * (1.0 + (ic - 1.0) * kap)
            kb = kk * ic
            ctot = ci[L - 1:L] if d == 0 else ci[0:1]
            pinv = jnp.exp(-ci)
            ptot = jnp.exp(ctot)
            bti = kb * pinv
            kti = kd * pinv
            rt.append((r * jnp.exp(ci)).astype(BF16))
            at.append((-kk * jnp.exp(ci - lw)).astype(BF16))
            bt.append(bti.astype(BF16))
            kt.append(kti.astype(BF16))
            bh.append((bti * ptot).astype(BF16))
            kh.append((kti * ptot).astype(BF16))
            vv.append(z(2).astype(BF16))
            pl_tot.append(ptot)

        cat = lambda xs: jnp.concatenate(xs, axis=1)
        rt, at, bt, kt, bh, kh, vv, pl_tot = map(cat, (rt, at, bt, kt, bh, kh, vv, pl_tot))
        bhkh = jnp.concatenate([bh, kh], axis=0)

        o1 = _dot(jnp.concatenate([at, rt], axis=0),
                  jnp.concatenate([bd(bt).T, bd(kt).T], axis=1))
        yield
        a_ab = jnp.where(strict, o1[0:L, 0:W4], 0.0)
        a_ak = jnp.where(strict, o1[0:L, W4:2 * W4], 0.0)
        a_r = jnp.concatenate([jnp.where(incl, o1[L:2 * L, 0:W4], 0.0),
                               jnp.where(incl, o1[L:2 * L, W4:2 * W4], 0.0)], axis=1).astype(BF16)

        dg = jnp.where(diag_blk, a_ab, 0.0)
        off = (a_ab - dg).astype(BF16)
        tn = eye + dg
        pw = mm(dg, dg)
        av = mm(a_ak, vv)
        yield
        for _ in range(SUB_LOG2 - 2):
            o = mm(jnp.concatenate([tn, pw], axis=0), pw)
            yield
            tn = tn + o[0:L]
            pw = o[L:2 * L]
        o = mm(tn, pw)
        yield
        td = tn + o
        f = mm(td, off)
        while c > 0 and (b, p, c - 1) not in state_written:
            yield
        ar = jnp.concatenate([at, rt], axis=0)
        o5 = jnp.concatenate(
            [_dot_nt(ar[:, d * LANES:(d + 1) * LANES], st_ref[b, p, d].astype(BF16))
             for d in range(2)], axis=1)
        yield
        o = mm(f, td)
        yield
        tinv = td + o
        o = mm(tinv, o5[0:L] + av)
        yield
        u = o.astype(BF16)
        o6 = _dot(a_r, jnp.concatenate([bd(u), bd(vv)], axis=0))
        uvv = jnp.concatenate([u, vv], axis=0)
        gs = [_dot_tn(uvv[:, d * LANES:(d + 1) * LANES], bhkh[:, d * LANES:(d + 1) * LANES])
              for d in range(2)]
        yield
        y = o5[L:2 * L] + o6
        for d in range(2):
            st_ref[b, p, d] = (st_ref[b, p, d] * pl_tot[:, d * LANES:(d + 1) * LANES]
                               + jnp.where(ones_bd_mask, gs[d], 0.0))
        state_written.add((b, p, c))
        yf_ref[p, b, rows[0]] = y[:, 0:LANES]
        yr_ref[p, b, rows[1]] = y[:, LANES:2 * LANES]

    _round_robin([pair_chain(b, p, c) for c in range(nsub) for b in range(nb) for p in range(PAIRS)])


def _wkv(z4, lora, w0, w2, a0, a2, k_k, k_a, r_k, *, batch, seq):
    tile = WKV_CHUNKS_PER_STEP * CHUNK
    nc = seq // tile
    z5 = z4.reshape(4, PAIRS, batch, seq, LANES)
    lora3 = lora.reshape(batch, seq, 4 * LORA)
    full = lambda shape: pl.BlockSpec(shape, lambda i: (0,) * len(shape))
    z_blk = (3, PAIRS, batch, tile, LANES)
    y_blk = (PAIRS, batch, tile, LANES)
    scr = (2, batch * WKV_CHUNKS_PER_STEP, PAIRS, CHUNK, LANES)
    yf, yr, bonus = pl.pallas_call(
        _wkv_kernel,
        grid=(nc,),
        in_specs=[
            pl.BlockSpec(z_blk, lambda i: (0, 0, 0, i, 0)),
            pl.BlockSpec(z_blk, lambda i: (0, 0, 0, nc - 1 - i, 0)),
            pl.BlockSpec((batch, tile, 4 * LORA), lambda i: (0, i, 0)),
            pl.BlockSpec((batch, tile, 4 * LORA), lambda i: (0, nc - 1 - i, 0)),
            full((2, 1, RWKV_W)), full((2, LORA, RWKV_W)),
            full((2, 1, RWKV_W)), full((2, LORA, RWKV_W)),
            full((PAIRS, 1, LANES)), full((PAIRS, 1, LANES)), full((PAIRS, 1, LANES)),
        ],
        out_specs=[
            pl.BlockSpec(y_blk, lambda i: (0, 0, i, 0)),
            pl.BlockSpec(y_blk, lambda i: (0, 0, nc - 1 - i, 0)),
            pl.BlockSpec(y_blk, lambda i: (0, 0, i, 0)),
        ],
        out_shape=[jax.ShapeDtypeStruct((PAIRS, batch, seq, LANES), F32)] * 3,
        scratch_shapes=[
            pltpu.VMEM((batch, PAIRS, 2, LANES, LANES), F32),
            pltpu.VMEM(scr, F32),
            pltpu.VMEM(scr, F32),
            pltpu.VMEM(scr, F32),
        ],
        compiler_params=pltpu.CompilerParams(
            dimension_semantics=("arbitrary",), vmem_limit_bytes=VMEM_LIMIT),
        name="wkv7",
    )(z5, z5, lora3, lora3, w0, w2, a0, a2, k_k, k_a, r_k)
    flat = lambda a: a.reshape(PAIRS, batch * seq, LANES)
    return flat(yf), flat(yr), flat(bonus)


NAT_ROWS_PER_STEP = 16
NAT_GROUP_W = 2 * LANES


def _nat_kernel(q_ref, k_ref, v_ref, g_ref, bias_ref, o_ref, *, rows):
    rb = pl.program_id(2)
    kh = min(NAT_KH, rows)
    win = kh * GRID_W
    scale = HEAD ** -0.5
    assert math.frexp(scale)[0] == 0.5, "scale is folded into bf16 q: must be a power of two"
    lane = lax.broadcasted_iota(jnp.int32, (GRID_W, LANES), 1)
    low = lane < HEAD

    def chain(qr, pp):
        r = rb * NAT_ROWS_PER_STEP + qr
        rs = jnp.clip(r - kh // 2, 0, rows - kh)
        pidx = rs - r + (NAT_KH - 1)
        q0 = qr * GRID_W
        k0 = pl.multiple_of(rs * GRID_W, GRID_W)
        sl = slice(pp * LANES, (pp + 1) * LANES)
        qp = q_ref[0, q0:q0 + GRID_W, sl]
        zq = jnp.zeros_like(qp)
        qs = jnp.concatenate([jnp.where(low, qp, zq), jnp.where(low, zq, qp)], axis=0)
        s = _dot_nt(qs, k_ref[0, pl.ds(k0, win), sl])
        yield
        s = s * scale + bias_ref[pidx, pp]
        m = jnp.max(s, axis=-1, keepdims=True)
        e = jnp.exp(s - m)
        l = jnp.sum(e, axis=-1, keepdims=True)
        o2 = _dot(e.astype(BF16), v_ref[pl.ds(k0, win), sl])
        yield
        o2 = o2 / l
        o = jnp.where(low, o2[0:GRID_W], o2[GRID_W:2 * GRID_W])
        g = g_ref[q0:q0 + GRID_W, sl]
        o_ref[q0:q0 + GRID_W, sl] = (o * (g * _sigmoid(g))).astype(o_ref.dtype)

    _round_robin([chain(qr, pp) for qr in range(NAT_ROWS_PER_STEP)
                  for pp in range(NAT_GROUP_W // LANES)])


def _nat(qk, vn, gn, bias, *, batch, seq):
    bt = gn.shape[0]
    rows = seq // GRID_W
    kh = min(NAT_KH, rows)
    tq = NAT_ROWS_PER_STEP * GRID_W
    nrb = rows // NAT_ROWS_PER_STEP
    ngrp = NAT_W // NAT_GROUP_W
    ppg = NAT_GROUP_W // LANES
    kernel = functools.partial(_nat_kernel, rows=rows)
    return pl.pallas_call(
        kernel,
        grid=(batch, ngrp, nrb),
        in_specs=[
            pl.BlockSpec((1, tq, NAT_GROUP_W), lambda b, g, r: (0, b * nrb + r, g)),
            pl.BlockSpec((1, seq, NAT_GROUP_W), lambda b, g, r: (1, b, g)),
            pl.BlockSpec((seq, NAT_GROUP_W), lambda b, g, r: (b, g)),
            pl.BlockSpec((tq, NAT_GROUP_W), lambda b, g, r: (b * nrb + r, g)),
            pl.BlockSpec((NAT_KH, ppg, 2 * GRID_W, kh * GRID_W), lambda b, g, r: (0, g, 0, 0)),
        ],
        out_specs=pl.BlockSpec((tq, NAT_GROUP_W), lambda b, g, r: (b * nrb + r, g)),
        out_shape=jax.ShapeDtypeStruct((bt, NAT_W), BF16),
        compiler_params=pltpu.CompilerParams(
            dimension_semantics=("arbitrary", "arbitrary", "arbitrary"), vmem_limit_bytes=VMEM_LIMIT),
        name="nat2d",
    )(qk, qk, vn, gn, bias)


def _nat_bias_table(rpb, rows):
    kh = min(NAT_KH, rows)
    heads, nro, nco = rpb.shape
    period = 2 * GRID_W
    u = jnp.concatenate([rpb[:, :, NAT_KW - 1:].astype(F32),
                         jnp.full((heads, nro, period - nco), NEG, F32),
                         rpb[:, :, :NAT_KW - 1].astype(F32)], axis=-1)
    flat = jnp.tile(u, (1, 1, GRID_W))[:, :, :GRID_W * (period - 1)]
    t = flat.reshape(heads, nro, GRID_W, period - 1)[:, :, :, :GRID_W]
    cols = np.arange(GRID_W)
    col_start = np.clip(cols - NAT_KW // 2, 0, GRID_W - NAT_KW)
    valid = (cols[None, :] >= col_start[:, None]) & (cols[None, :] < col_start[:, None] + NAT_KW)
    t = jnp.where(valid[None, None], t, NEG)
    t = jnp.transpose(t, (0, 2, 1, 3)).reshape(heads // 2, 2 * GRID_W, nro * GRID_W)
    return jnp.stack([t[:, :, pi * GRID_W:(pi + kh) * GRID_W] for pi in range(NAT_KH)])


def _out_kernel(yf_ref, yr_ref, bonus_ref, grw_ref, yb_ref, x_ref, p_ref, lnw_ref, lnb_ref,
                wout_ref, pg_ref, wgate_ref, wproj_ref, fg_ref, o_ref):
    r1 = lax.broadcasted_iota(jnp.int32, (LANES, LANES), 0)
    c1 = lax.broadcasted_iota(jnp.int32, (LANES, LANES), 1)
    avg_bd = jnp.where((r1 // HEAD) == (c1 // HEAD), 1.0 / HEAD, 0.0).astype(BF16)
    parts = []
    for p in range(PAIRS):
        y = yf_ref[p] + yr_ref[p]
        mu = _dot(y.astype(BF16), avg_bd)
        dlt = y - mu
        var = _dot((dlt * dlt).astype(BF16), avg_bd)
        yn = dlt * lax.rsqrt(var + LNX_EPS) * lnw_ref[p] + lnb_ref[p]
        g = grw_ref[0, p]
        parts.append(((yn + bonus_ref[p]) * (g * _sigmoid(g))).astype(BF16))
    parts.append(yb_ref[...])
    mix = jnp.concatenate(parts, axis=1)
    h = x_ref[...] + _dot(mix, wout_ref[...])

    def rms(v, gain):
        ms = jnp.mean(v * v, axis=-1, keepdims=True)
        return v * lax.rsqrt(ms + NORM_EPS) * gain

    gate = _sigmoid(_dot(rms(h, pg_ref[...]).astype(BF16), wgate_ref[...]))
    h = h + _dot(p_ref[...].astype(BF16), wproj_ref[...]) * gate
    o_ref[...] = rms(h, fg_ref[...])


def _out_stage(yf, yr, bonus, z4, yb, x2, p2, lnw, lnb, w_out, ple_g, w_gate, w_proj, final_g, *, tm):
    bt, d = x2.shape
    ple = p2.shape[1]
    const = lambda shape: pl.BlockSpec(shape, lambda i: (0,) * len(shape), pipeline_mode=pl.Buffered(1))
    pair_spec = pl.BlockSpec((PAIRS, tm, LANES), lambda i: (0, i, 0))
    return pl.pallas_call(
        _out_kernel,
        grid=(bt // tm,),
        in_specs=[
            pair_spec, pair_spec, pair_spec,
            pl.BlockSpec((1, PAIRS, tm, LANES), lambda i: (3, 0, i, 0)),
            pl.BlockSpec((tm, NAT_W), lambda i: (i, 0)),
            pl.BlockSpec((tm, d), lambda i: (i, 0)),
            pl.BlockSpec((tm, ple), lambda i: (i, 0)),
            const((PAIRS, 1, LANES)), const((PAIRS, 1, LANES)),
            const((RWKV_W + NAT_W, d)), const((1, d)), const((d, d)), const((ple, d)), const((1, d)),
        ],
        out_specs=pl.BlockSpec((tm, d), lambda i: (i, 0)),
        out_shape=jax.ShapeDtypeStruct((bt, d), F32),
        compiler_params=pltpu.CompilerParams(
            dimension_semantics=("arbitrary",), vmem_limit_bytes=VMEM_LIMIT),
        name="out_stage",
    )(yf, yr, bonus, z4, yb, x2, p2, lnw, lnb, w_out, ple_g, w_gate, w_proj, final_g)


INPROJ_TM = 512
OUT_TM = 256


def kernel(x, p, norm_mix_g, w_in, shift_mu_prev, shift_mu_next, decay_w0, decay_w2, iclr_a0, iclr_a2,
           k_k, k_a, r_k, lnx_w, lnx_b, nat_rpb, w_out, ple_norm_g, w_ple_gate, w_ple_proj, final_norm_g):
    batch, seq, d = x.shape
    depth = p.shape[0]
    bt = batch * seq
    rows = seq // GRID_W
    o_wd = 3 * RWKV_W
    shift_w = o_wd + 4 * LORA
    pairs = lambda a: a.reshape(PAIRS, 1, LANES).astype(F32)

    assert depth == 1, "the output stage fuses the final norm: one trunk layer only"
    h = x.reshape(bt, d)
    w = w_in[0]
    w_main = _drop_columns_bf16(w, o_wd, shift_w, 4 * LORA)
    w_lora = w[:, o_wd:shift_w].astype(BF16)
    mup, mun = shift_mu_prev[0], shift_mu_next[0]
    z4, qk, vn, gn, lora = _inproj(
        h, norm_mix_g[0].reshape(1, d), w_main, w_lora,
        mup[:o_wd].reshape(3, 1, RWKV_W), mun[:o_wd].reshape(3, 1, RWKV_W),
        mup[o_wd:].reshape(1, 4 * LORA), mun[o_wd:].reshape(1, 4 * LORA),
        seq=seq, tm=INPROJ_TM)
    yf, yr, bonus = _wkv(
        z4, lora,
        decay_w0[0].reshape(2, 1, RWKV_W), decay_w2[0].astype(BF16),
        iclr_a0[0].reshape(2, 1, RWKV_W), iclr_a2[0].astype(BF16),
        pairs(k_k[0]), pairs(k_a[0]), pairs(r_k[0]), batch=batch, seq=seq)
    yb = _nat(qk, vn, gn, _nat_bias_table(nat_rpb[0], rows), batch=batch, seq=seq)
    out = _out_stage(
        yf, yr, bonus, z4, yb, h, p[0].reshape(bt, -1), pairs(lnx_w[0]), pairs(lnx_b[0]),
        w_out[0].astype(BF16), ple_norm_g[0].reshape(1, d), w_ple_gate[0].astype(BF16),
        w_ple_proj[0].astype(BF16), final_norm_g.reshape(1, d), tm=OUT_TM)
    return out.reshape(batch, seq, d)
```
